```python
import jax, jax.numpy as jnp
from jax import lax
import numpy as np

D_MODEL = 2048
BATCH = 8
SEQ = 2048
DEPTH = 4

HEAD_DIM = 128
DIL_GROUPS = ((128, 1), (512, 4), (2048, 16))
HEADS_PER_DIL = 4
N_DIL_HEADS = HEADS_PER_DIL * len(DIL_GROUPS)
N_FOX_HEADS = 4
N_HEADS = N_DIL_HEADS + N_FOX_HEADS
ATTN_WIDTH = N_HEADS * HEAD_DIM
BRANCH_A_WIDTH = HEADS_PER_DIL * HEAD_DIM
BRANCH_B_WIDTH = N_FOX_HEADS * HEAD_DIM
IN_COLS = 3 * ATTN_WIDTH + N_FOX_HEADS
D_FF = 4 * D_MODEL
PLE_DIM = 256
ROPE_THETA = 500000.0
ROPE_DIM = HEAD_DIM // 4
BLOCK = 128
NORM_EPS = 1e-6

kernel_name = "hybrid_dilated_fox_gated_block"


def rms_norm(x, g):
    xf = x.astype(jnp.float32)
    y = xf * lax.rsqrt(jnp.mean(xf * xf, axis=-1, keepdims=True) + NORM_EPS)
    return (y * g.astype(jnp.float32)).astype(x.dtype)


def partial_rope(x):
    S = x.shape[1]
    half = ROPE_DIM // 2
    inv = ROPE_THETA ** (-jnp.arange(half, dtype=jnp.float32) / half)
    ang = jnp.arange(S, dtype=jnp.float32)[:, None] * inv[None, :]
    cos = jnp.cos(ang)[None, :, None, :]
    sin = jnp.sin(ang)[None, :, None, :]
    xr = x[..., :ROPE_DIM].astype(jnp.float32)
    x1, x2 = xr[..., :half], xr[..., half:]
    rot = jnp.concatenate([x1 * cos - x2 * sin, x2 * cos + x1 * sin], axis=-1).astype(x.dtype)
    return jnp.concatenate([rot, x[..., ROPE_DIM:]], axis=-1)


def dilated_group_attention(q, k, v, window, dilation):
    B, S, H, Dh = q.shape
    span = window // dilation
    L = S // dilation
    nb = -(-L // BLOCK)
    Lp = nb * BLOCK

    def to_blocks(t):
        t = t.reshape(B, L, dilation, H, Dh).transpose(0, 2, 1, 3, 4)
        t = jnp.pad(t, ((0, 0), (0, 0), (0, Lp - L), (0, 0), (0, 0)))
        return t.reshape(B, dilation, nb, BLOCK, H, Dh)

    def with_prev(t):
        prev = jnp.pad(t[:, :, :-1], ((0, 0), (0, 0), (1, 0), (0, 0), (0, 0), (0, 0)))
        return jnp.concatenate([prev, t], axis=3)

    qb = to_blocks(q)
    kb = with_prev(to_blocks(k))
    vb = with_prev(to_blocks(v))
    s = jnp.einsum("brnqhd,brnkhd->brnhqk", qb, kb).astype(jnp.float32) * (Dh ** -0.5)
    qi = jnp.arange(BLOCK)[:, None]
    ki = jnp.arange(2 * BLOCK)[None, :]
    dist = BLOCK + qi - ki
    band = (dist >= 0) & (dist <= span)
    key_exists = (jnp.arange(nb) > 0)[:, None, None] | (ki >= BLOCK)[None]
    mask = (band[None] & key_exists)[:, None]
    s = jnp.where(mask, s, -jnp.inf)
    lse = jax.nn.logsumexp(s, axis=-1)
    prob = jnp.exp(s - lse[..., None]).astype(v.dtype)
    o = jnp.einsum("brnhqk,brnkhd->brnqhd", prob, vb)
    o = o.reshape(B, dilation, Lp, H, Dh)[:, :, :L].transpose(0, 2, 1, 3, 4).reshape(B, S, H, Dh)
    lse = lse.transpose(0, 1, 2, 4, 3).reshape(B, dilation, Lp, H)[:, :, :L]
    lse = lse.transpose(0, 2, 1, 3).reshape(B, S, H)
    return o, lse


def dilated_mixture(q, k, v):
    outs, lses = [], []
    for g, (window, dilation) in enumerate(DIL_GROUPS):
        sl = slice(g * HEADS_PER_DIL, (g + 1) * HEADS_PER_DIL)
        o, l = dilated_group_attention(q[:, :, sl], k[:, :, sl], v[:, :, sl], window, dilation)
        outs.append(o)
        lses.append(l)
    o = jnp.stack(outs, axis=0)
    w = jax.nn.softmax(jnp.stack(lses, axis=0), axis=0)
    return jnp.sum(w[..., None].astype(o.dtype) * o, axis=0)


def forgetting_attention(q, k, v, f_logit):
    B, S, H, Dh = q.shape
    nb = S // BLOCK
    c = jnp.cumsum(jax.nn.log_sigmoid(f_logit.astype(jnp.float32)), axis=1)
    c_keys = c.transpose(0, 2, 1)[:, :, None, :]
    qb = q.reshape(B, nb, BLOCK, H, Dh).transpose(1, 0, 2, 3, 4)
    cb = c.reshape(B, nb, BLOCK, H).transpose(1, 0, 3, 2)
    kpos = jnp.arange(S)
    scale = Dh ** -0.5

    def one_block(args):
        j, qj, cj = args
        s = jnp.einsum("bqhd,bkhd->bhqk", qj, k).astype(jnp.float32) * scale
        s = s + cj[..., None] - c_keys
        qpos = j * BLOCK + jnp.arange(BLOCK)
        s = jnp.where(kpos[None, :] <= qpos[:, None], s, -jnp.inf)
        prob = jax.nn.softmax(s, axis=-1).astype(v.dtype)
        return jnp.einsum("bhqk,bkhd->bqhd", prob, v)

    out = lax.map(one_block, (jnp.arange(nb), qb, cb))
    return out.transpose(1, 0, 2, 3, 4).reshape(B, S, H, Dh)


def hybrid_layer(h, p_i, g_mix, w_in, b_f, w_gate, b_gate, w_br_a, w_br_b, w_o,
                 g_mlp, w_up, w_down, g_ple, w_ple, w_ple_gate):
    B, S, _ = h.shape
    u = rms_norm(h, g_mix)
    z = u @ w_in
    q = z[..., :ATTN_WIDTH].reshape(B, S, N_HEADS, HEAD_DIM)
    k = z[..., ATTN_WIDTH:2 * ATTN_WIDTH].reshape(B, S, N_HEADS, HEAD_DIM)
    v = z[..., 2 * ATTN_WIDTH:3 * ATTN_WIDTH].reshape(B, S, N_HEADS, HEAD_DIM)
    f_logit = z[..., 3 * ATTN_WIDTH:] + b_f

    ya = dilated_mixture(partial_rope(q[:, :, :N_DIL_HEADS]), partial_rope(k[:, :, :N_DIL_HEADS]),
                         v[:, :, :N_DIL_HEADS])
    ya = ya.reshape(B, S, BRANCH_A_WIDTH) @ w_br_a
    yb = forgetting_attention(q[:, :, N_DIL_HEADS:], k[:, :, N_DIL_HEADS:], v[:, :, N_DIL_HEADS:], f_logit)
    yb = yb.reshape(B, S, BRANCH_B_WIDTH) @ w_br_b

    gates = jax.nn.sigmoid(u @ w_gate + b_gate)
    merged = gates[..., :D_MODEL] * ya + gates[..., D_MODEL:] * yb
    h = h + merged @ w_o

    m = rms_norm(h, g_mlp)
    h = h + jnp.square(jax.nn.relu(m @ w_up)) @ w_down

    ple_gate = jax.nn.sigmoid(rms_norm(h, g_ple) @ w_ple_gate)
    h = h + ple_gate * (p_i @ w_ple)
    return h


def setup_inputs(seed: int = 0) -> dict:
    key = jax.random.key(seed)
    ks = jax.random.split(key, 20)
    f32 = jnp.float32

    def w(k, shape, fan_in):
        return jax.random.normal(k, shape, f32) * (fan_in ** -0.5)

    def gain(k, shape):
        return 1.0 + 0.02 * jax.random.normal(k, shape, f32)

    return {
        "x": jax.random.normal(ks[0], (BATCH, SEQ, D_MODEL), f32),
        "p": jax.random.normal(ks[1], (DEPTH, BATCH, SEQ, PLE_DIM), f32),
        "g_mix": gain(ks[2], (DEPTH, D_MODEL)),
        "w_in": w(ks[3], (DEPTH, D_MODEL, IN_COLS), D_MODEL),
        "b_f": 3.0 + 0.1 * jax.random.normal(ks[4], (DEPTH, N_FOX_HEADS), f32),
        "w_gate": w(ks[5], (DEPTH, D_MODEL, 2 * D_MODEL), D_MODEL),
        "b_gate": 0.1 * jax.random.normal(ks[6], (DEPTH, 2 * D_MODEL), f32),
        "w_br_a": w(ks[7], (DEPTH, BRANCH_A_WIDTH, D_MODEL), BRANCH_A_WIDTH),
        "w_br_b": w(ks[8], (DEPTH, BRANCH_B_WIDTH, D_MODEL), BRANCH_B_WIDTH),
        "w_o": w(ks[9], (DEPTH, D_MODEL, D_MODEL), D_MODEL),
        "g_mlp": gain(ks[10], (DEPTH, D_MODEL)),
        "w_up": w(ks[11], (DEPTH, D_MODEL, D_FF), D_MODEL),
        "w_down": w(ks[12], (DEPTH, D_FF, D_MODEL), D_FF),
        "g_ple": gain(ks[13], (DEPTH, D_MODEL)),
        "w_ple": w(ks[14], (DEPTH, PLE_DIM, D_MODEL), PLE_DIM),
        "w_ple_gate": w(ks[15], (DEPTH, D_MODEL, D_MODEL), D_MODEL),
        "g_final": gain(ks[16], (D_MODEL,)),
    }


def reference(x, p, g_mix, w_in, b_f, w_gate, b_gate, w_br_a, w_br_b, w_o,
              g_mlp, w_up, w_down, g_ple, w_ple, w_ple_gate, g_final):
    h = x
    for i in range(DEPTH):
        h = hybrid_layer(h, p[i], g_mix[i], w_in[i], b_f[i], w_gate[i], b_gate[i], w_br_a[i], w_br_b[i],
                         w_o[i], g_mlp[i], w_up[i], w_down[i], g_ple[i], w_ple[i], w_ple_gate[i])
    return rms_norm(h, g_final)
```

```python
import functools

import jax
import jax.numpy as jnp
from jax import lax
from jax.experimental import pallas as pl
from jax.experimental.pallas import tpu as pltpu

HEAD_DIM = 128
BLOCK = 128
DIL_GROUPS = ((128, 1), (512, 4), (2048, 16))
HEADS_PER_DIL = 4
N_DIL_HEADS = HEADS_PER_DIL * len(DIL_GROUPS)
N_FOX_HEADS = 4
N_HEADS = N_DIL_HEADS + N_FOX_HEADS
ATTN_WIDTH = N_HEADS * HEAD_DIM
GROUP_WIDTH = HEADS_PER_DIL * HEAD_DIM
DIL_WIDTH = N_DIL_HEADS * HEAD_DIM
ROPE_THETA = 500000.0
ROPE_DIM = HEAD_DIM // 4
ROPE_HALF = ROPE_DIM // 2
NORM_EPS = 1e-6
ATTN_SCALE = HEAD_DIM ** -0.5

LANES = 128
SUBLANES = 8
V7X_VMEM_LIMIT_CAP = 56 * 1024 * 1024

F32 = jnp.float32
BF16 = jnp.bfloat16


def _nbytes(shape, dtype):
    n = 1
    for s in shape:
        n *= s
    return n * jnp.dtype(dtype).itemsize


def _compiler_params(semantics, pipelined, resident=0):
    need = 2 * pipelined + resident
    assert need <= V7X_VMEM_LIMIT_CAP, (need, V7X_VMEM_LIMIT_CAP)
    return pltpu.CompilerParams(dimension_semantics=semantics, vmem_limit_bytes=V7X_VMEM_LIMIT_CAP)


def _rms(x, g):
    return x * lax.rsqrt(jnp.mean(x * x, axis=-1, keepdims=True) + NORM_EPS) * g


def _sigmoid(x):
    return 1.0 / (1.0 + jnp.exp(-x))


def _rms_kernel(x_ref, g_ref, o_ref):
    o_ref[...] = _rms(x_ref[...], g_ref[...]).astype(o_ref.dtype)


def _rms_call(x, g, tm):
    T, D = x.shape
    return pl.pallas_call(
        _rms_kernel,
        grid=(T // tm,),
        in_specs=[pl.BlockSpec((tm, D), lambda i: (i, 0)),
                  pl.BlockSpec((1, D), lambda i: (0, 0))],
        out_specs=pl.BlockSpec((tm, D), lambda i: (i, 0)),
        out_shape=jax.ShapeDtypeStruct((T, D), BF16),
        compiler_params=_compiler_params(("parallel",), _nbytes((tm, D), F32) + _nbytes((tm, D), BF16)),
        name="rms_in",
    )(x, g)


def _in_proj_kernel(u_ref, w_ref, b_ref, cos_ref, sa_ref, sb_ref, o_ref, *, rope_tiles, plain_tiles):
    j = pl.program_id(1)
    acc = jnp.dot(u_ref[...], w_ref[...], preferred_element_type=F32)

    @pl.when(j < rope_tiles)
    def _rope():
        c, sa, sb = cos_ref[...], sa_ref[...], sb_ref[...]
        for hh in range(acc.shape[1] // HEAD_DIM):
            x = acc[:, hh * HEAD_DIM:(hh + 1) * HEAD_DIM]
            y = x * c + pltpu.roll(x, ROPE_HALF, 1) * sa + pltpu.roll(x, HEAD_DIM - ROPE_HALF, 1) * sb
            o_ref[:, hh * HEAD_DIM:(hh + 1) * HEAD_DIM] = y.astype(o_ref.dtype)

    @pl.when((j >= rope_tiles) & (j < rope_tiles + plain_tiles))
    def _plain():
        o_ref[...] = acc.astype(o_ref.dtype)

    @pl.when(j >= rope_tiles + plain_tiles)
    def _gate():
        o_ref[...] = _sigmoid(acc + b_ref[...]).astype(o_ref.dtype)


def _in_proj_call(u, w, b, cos_t, sa_t, sb_t, layer, S, tm, tn):
    T, D = u.shape
    N = w.shape[2]
    rope_tiles = (2 * DIL_WIDTH) // tn
    plain_tiles = (3 * ATTN_WIDTH - 2 * DIL_WIDTH) // tn
    pos_blocks = S // tm
    blocks = (_nbytes((tm, D), BF16) + _nbytes((D, tn), BF16) + _nbytes((tm, tn), BF16)
              + 3 * _nbytes((tm, LANES), F32))
    return pl.pallas_call(
        functools.partial(_in_proj_kernel, rope_tiles=rope_tiles, plain_tiles=plain_tiles),
        grid=(T // tm, N // tn),
        in_specs=[pl.BlockSpec((tm, D), lambda i, j: (i, 0)),
                  pl.BlockSpec((None, D, tn), lambda i, j: (layer, 0, j)),
                  pl.BlockSpec((None, 1, tn), lambda i, j: (layer, 0, j)),
                  pl.BlockSpec((tm, LANES), lambda i, j: (i % pos_blocks, 0)),
                  pl.BlockSpec((tm, LANES), lambda i, j: (i % pos_blocks, 0)),
                  pl.BlockSpec((tm, LANES), lambda i, j: (i % pos_blocks, 0))],
        out_specs=pl.BlockSpec((tm, tn), lambda i, j: (i, j)),
        out_shape=jax.ShapeDtypeStruct((T, N), BF16),
        compiler_params=_compiler_params(("parallel", "arbitrary"), blocks, 2 * _nbytes((tm, tn), F32)),
        name="in_proj",
    )(u, w, b, cos_t, sa_t, sb_t)


def _decay_kernel(u_ref, w_ref, b_ref, ccol_ref, crow_ref):
    fl = jnp.dot(u_ref[...], w_ref[...], preferred_element_type=F32) + b_ref[...]
    ls = jnp.minimum(fl, 0.0) - jnp.log1p(jnp.exp(-jnp.abs(fl)))
    ri = lax.broadcasted_iota(jnp.int32, (BLOCK, BLOCK), 0)
    ci = lax.broadcasted_iota(jnp.int32, (BLOCK, BLOCK), 1)
    tri = (ci <= ri).astype(F32)
    carry = jnp.zeros((1, LANES), F32)
    for n in range(fl.shape[0] // BLOCK):
        rows = slice(n * BLOCK, (n + 1) * BLOCK)
        blk = jnp.dot(tri, ls[rows], precision=lax.Precision.HIGHEST, preferred_element_type=F32) + carry
        ccol_ref[0, rows, :] = blk
        crow_ref[0, :, rows] = blk.T[0:SUBLANES, :]
        carry = blk[BLOCK - 1:BLOCK, :]


def _decay_call(u, w, b, layer, B, S):
    T, D = u.shape
    blocks = (_nbytes((S, D), BF16) + _nbytes((D, LANES), BF16) + _nbytes((S, LANES), F32)
              + _nbytes((SUBLANES, S), F32))
    return pl.pallas_call(
        _decay_kernel,
        grid=(B,),
        in_specs=[pl.BlockSpec((S, D), lambda b_: (b_, 0)),
                  pl.BlockSpec((None, D, LANES), lambda b_: (layer, 0, 0)),
                  pl.BlockSpec((None, 1, LANES), lambda b_: (layer, 0, 0))],
        out_specs=[pl.BlockSpec((1, S, LANES), lambda b_: (b_, 0, 0)),
                   pl.BlockSpec((1, SUBLANES, S), lambda b_: (b_, 0, 0))],
        out_shape=[jax.ShapeDtypeStruct((B, S, LANES), F32),
                   jax.ShapeDtypeStruct((B, SUBLANES, S), F32)],
        compiler_params=_compiler_params(("parallel",), blocks, 4 * _nbytes((S, LANES), F32)),
        name="fox_decay",
    )(u, w, b)


def _dil_kernel(q_ref, k_ref, v_ref, o_ref, lse_ref, *, nb):
    qi = lax.broadcasted_iota(jnp.int32, (BLOCK, 2 * BLOCK), 0)
    ki = lax.broadcasted_iota(jnp.int32, (BLOCK, 2 * BLOCK), 1)
    band = (ki >= qi) & (ki <= qi + BLOCK)
    qd = lax.broadcasted_iota(jnp.int32, (BLOCK, BLOCK), 0)
    kd = lax.broadcasted_iota(jnp.int32, (BLOCK, BLOCK), 1)
    diag = kd <= qd

    def attend(q, k, v, mask):
        s = lax.dot_general(q, k, (((1,), (1,)), ((), ())), preferred_element_type=F32) * ATTN_SCALE
        s = jnp.where(mask, s, -jnp.inf)
        m = jnp.max(s, axis=-1, keepdims=True)
        p = jnp.exp(s - m)
        l = jnp.sum(p, axis=-1, keepdims=True)
        o = jnp.dot(p.astype(v.dtype), v, preferred_element_type=F32) * (1.0 / l)
        return o, m + jnp.log(l)

    for h in range(HEADS_PER_DIL):
        cs = pl.ds(h * HEAD_DIM, HEAD_DIM)
        o, lse = attend(q_ref[0, 0:BLOCK, cs], k_ref[0, 0:BLOCK, cs], v_ref[0, 0:BLOCK, cs], diag)
        o_ref[0, 0:BLOCK, cs] = o
        lse_ref[0, 0:BLOCK, cs] = jnp.broadcast_to(lse, (BLOCK, HEAD_DIM))

        def body(n, carry, cs=cs):
            r0 = pl.multiple_of(n * BLOCK, BLOCK)
            rk = pl.multiple_of(n * BLOCK - BLOCK, BLOCK)
            o, lse = attend(q_ref[0, pl.ds(r0, BLOCK), cs], k_ref[0, pl.ds(rk, 2 * BLOCK), cs],
                            v_ref[0, pl.ds(rk, 2 * BLOCK), cs], band)
            o_ref[0, pl.ds(r0, BLOCK), cs] = o
            lse_ref[0, pl.ds(r0, BLOCK), cs] = jnp.broadcast_to(lse, (BLOCK, HEAD_DIM))
            return carry

        if nb > 1:
            lax.fori_loop(1, nb, body, 0)


def _dil_call(z, g, dilation, B, S):
    T, N = z.shape
    L = S // dilation
    nb = L // BLOCK
    zv = z.reshape(B, L, dilation * N)
    cpr = N // GROUP_WIDTH
    q_blk, k_blk, v_blk = g, DIL_WIDTH // GROUP_WIDTH + g, 2 * DIL_WIDTH // GROUP_WIDTH + g

    def spec(cb):
        return pl.BlockSpec((1, L, GROUP_WIDTH), lambda b_, r: (b_, 0, r * cpr + cb))

    out_spec = pl.BlockSpec((1, L, GROUP_WIDTH), lambda b_, r: (b_, 0, r))
    out_sds = jax.ShapeDtypeStruct((B, L, dilation * GROUP_WIDTH), F32)
    blocks = 3 * _nbytes((L, GROUP_WIDTH), BF16) + 2 * _nbytes((L, GROUP_WIDTH), F32)
    o, lse = pl.pallas_call(
        functools.partial(_dil_kernel, nb=nb),
        grid=(B, dilation),
        in_specs=[spec(q_blk), spec(k_blk), spec(v_blk)],
        out_specs=[out_spec, out_spec],
        out_shape=[out_sds, out_sds],
        compiler_params=_compiler_params(("parallel", "parallel"), blocks),
        name=f"dilated_d{dilation}",
    )(zv, zv, zv)
    return o.reshape(T, GROUP_WIDTH), lse.reshape(T, GROUP_WIDTH)


def _fox_kernel(q_ref, k_ref, v_ref, cq_ref, ck_ref, o_ref, *, tq, tk):
    i = pl.program_id(1)
    qpos = i * tq + lax.broadcasted_iota(jnp.int32, (tq, tk), 0)
    kofs = lax.broadcasted_iota(jnp.int32, (tq, tk), 1)
    n_tiles = (i * tq) // tk + tq // tk

    for h in range(N_FOX_HEADS):
        cs = pl.ds(h * HEAD_DIM, HEAD_DIM)
        q = q_ref[0, :, cs]
        cq = cq_ref[0, :, h:h + 1]

        def body(t, carry, cs=cs, h=h, q=q, cq=cq):
            m, l, acc = carry
            k0 = pl.multiple_of(t * tk, tk)
            k = k_ref[0, pl.ds(k0, tk), cs]
            v = v_ref[0, pl.ds(k0, tk), cs]
            ck = ck_ref[0, h:h + 1, pl.ds(k0, tk)]
            s = lax.dot_general(q, k, (((1,), (1,)), ((), ())), preferred_element_type=F32) * ATTN_SCALE
            s = s + (cq - ck)
            s = jnp.where(kofs + k0 <= qpos, s, -jnp.inf)
            m_new = jnp.maximum(m, jnp.max(s, axis=-1, keepdims=True))
            alpha = jnp.exp(m - m_new)
            p = jnp.exp(s - m_new)
            l = alpha * l + jnp.sum(p, axis=-1, keepdims=True)
            acc = alpha * acc + jnp.dot(p.astype(v.dtype), v, preferred_element_type=F32)
            return m_new, l, acc

        init = (jnp.full((tq, 1), -jnp.inf, F32), jnp.zeros((tq, 1), F32), jnp.zeros((tq, HEAD_DIM), F32))
        _, l, acc = lax.fori_loop(0, n_tiles, body, init)
        o_ref[0, :, cs] = (acc * (1.0 / l)).astype(o_ref.dtype)


def _fox_call(z, ccol, crow, B, S, tq, tk):
    T, N = z.shape
    zv = z.reshape(B, S, N)
    base = 3 * DIL_WIDTH // GROUP_WIDTH
    blocks = (_nbytes((tq, GROUP_WIDTH), BF16) * 2 + 2 * _nbytes((S, GROUP_WIDTH), BF16)
              + _nbytes((tq, LANES), F32) + _nbytes((SUBLANES, S), F32))
    out = pl.pallas_call(
        functools.partial(_fox_kernel, tq=tq, tk=tk),
        grid=(B, S // tq),
        in_specs=[pl.BlockSpec((1, tq, GROUP_WIDTH), lambda b_, i: (b_, i, base)),
                  pl.BlockSpec((1, S, GROUP_WIDTH), lambda b_, i: (b_, 0, base + 1)),
                  pl.BlockSpec((1, S, GROUP_WIDTH), lambda b_, i: (b_, 0, base + 2)),
                  pl.BlockSpec((1, tq, LANES), lambda b_, i: (b_, i, 0)),
                  pl.BlockSpec((1, SUBLANES, S), lambda b_, i: (b_, 0, 0))],
        out_specs=pl.BlockSpec((1, tq, GROUP_WIDTH), lambda b_, i: (b_, i, 0)),
        out_shape=jax.ShapeDtypeStruct((B, S, GROUP_WIDTH), BF16),
        compiler_params=_compiler_params(("parallel", "arbitrary"), blocks, 4 * _nbytes((tq, tk), F32)),
        name="fox_attention",
    )(zv, zv, zv, ccol, crow)
    return out.reshape(T, GROUP_WIDTH)


def _merge_kernel(o0_ref, o1_ref, o2_ref, l0_ref, l1_ref, l2_ref, yb_ref, ga_ref, gb_ref, h_ref,
                  wa_ref, wb_ref, wo_ref, h1_ref):
    l0, l1, l2 = l0_ref[...], l1_ref[...], l2_ref[...]
    mx = jnp.maximum(jnp.maximum(l0, l1), l2)
    e0, e1, e2 = jnp.exp(l0 - mx), jnp.exp(l1 - mx), jnp.exp(l2 - mx)
    inv = 1.0 / (e0 + e1 + e2)
    ya = (e0 * inv) * o0_ref[...] + (e1 * inv) * o1_ref[...] + (e2 * inv) * o2_ref[...]
    ya_p = jnp.dot(ya.astype(BF16), wa_ref[...], preferred_element_type=F32)
    yb_p = jnp.dot(yb_ref[...], wb_ref[...], preferred_element_type=F32)
    merged = ga_ref[...].astype(F32) * ya_p + gb_ref[...].astype(F32) * yb_p
    h1_ref[...] = h_ref[...] + jnp.dot(merged.astype(BF16), wo_ref[...], preferred_element_type=F32)


def _merge_call(os_, ls_, yb, z, h, wa, wb, wo, layer, tm):
    T, D = h.shape
    gate_blk = (3 * ATTN_WIDTH) // D
    row = lambda i: (i, 0)
    gw = pl.BlockSpec((tm, GROUP_WIDTH), row)
    blocks = (6 * _nbytes((tm, GROUP_WIDTH), F32) + _nbytes((tm, GROUP_WIDTH), BF16) + 2 * _nbytes((tm, D), BF16)
              + 2 * _nbytes((tm, D), F32) + 2 * _nbytes((GROUP_WIDTH, D), BF16) + _nbytes((D, D), BF16))
    return pl.pallas_call(
        _merge_kernel,
        grid=(T // tm,),
        in_specs=[gw, gw, gw, gw, gw, gw, gw,
                  pl.BlockSpec((tm, D), lambda i: (i, gate_blk)),
                  pl.BlockSpec((tm, D), lambda i: (i, gate_blk + 1)),
                  pl.BlockSpec((tm, D), row),
                  pl.BlockSpec((None, GROUP_WIDTH, D), lambda i: (layer, 0, 0)),
                  pl.BlockSpec((None, GROUP_WIDTH, D), lambda i: (layer, 0, 0)),
                  pl.BlockSpec((None, D, D), lambda i: (layer, 0, 0))],
        out_specs=pl.BlockSpec((tm, D), row),
        out_shape=jax.ShapeDtypeStruct((T, D), F32),
        compiler_params=_compiler_params(("parallel",), blocks, 3 * _nbytes((tm, D), F32)),
        name="merge_wo",
    )(*os_, *ls_, yb, z, z, h, wa, wb, wo)


def _mlp_kernel(h_ref, g_ref, wu_ref, wd_ref, o_ref, m_ref):
    j = pl.program_id(1)

    @pl.when(j == 0)
    def _init():
        h = h_ref[...]
        m_ref[...] = _rms(h, g_ref[...]).astype(m_ref.dtype)
        o_ref[...] = h

    a = jnp.dot(m_ref[...], wu_ref[...], preferred_element_type=F32)
    a = jnp.square(jnp.maximum(a, 0.0)).astype(BF16)
    o_ref[...] += jnp.dot(a, wd_ref[...], preferred_element_type=F32)


def _mlp_call(h, g, wu, wd, layer, tm, tf):
    T, D = h.shape
    FF = wu.shape[2]
    blocks = 2 * _nbytes((tm, D), F32) + 2 * _nbytes((D, tf), BF16)
    resident = _nbytes((tm, D), BF16) + _nbytes((tm, tf), F32) + _nbytes((tm, tf), BF16)
    return pl.pallas_call(
        _mlp_kernel,
        grid=(T // tm, FF // tf),
        in_specs=[pl.BlockSpec((tm, D), lambda i, j: (i, 0)),
                  pl.BlockSpec((None, 1, D), lambda i, j: (layer, 0, 0)),
                  pl.BlockSpec((None, D, tf), lambda i, j: (layer, 0, j)),
                  pl.BlockSpec((None, tf, D), lambda i, j: (layer, j, 0))],
        out_specs=pl.BlockSpec((tm, D), lambda i, j: (i, 0)),
        out_shape=jax.ShapeDtypeStruct((T, D), F32),
        scratch_shapes=[pltpu.VMEM((tm, D), BF16)],
        compiler_params=_compiler_params(("parallel", "arbitrary"), blocks, resident),
        name="mlp",
    )(h, g, wu, wd)


def _ple_kernel(h_ref, p_ref, g_ref, gn_ref, wg_ref, wp_ref, *out_refs, emit_h):
    h = h_ref[...]
    n = _rms(h, g_ref[...]).astype(BF16)
    gate = _sigmoid(jnp.dot(n, wg_ref[...], preferred_element_type=F32))
    pp = jnp.dot(p_ref[...].astype(BF16), wp_ref[...], preferred_element_type=F32)
    h3 = h + gate * pp
    if emit_h:
        out_refs[0][...] = h3
        out_refs[1][...] = _rms(h3, gn_ref[...]).astype(out_refs[1].dtype)
    else:
        out_refs[0][...] = _rms(h3, gn_ref[...]).astype(out_refs[0].dtype)


def _ple_call(h, p, g, g_next, wg, wp, layer, tm, last):
    T, D = h.shape
    P = p.shape[2]
    row = lambda i: (i, 0)
    if last:
        out_specs = [pl.BlockSpec((tm, D), row)]
        out_shape = [jax.ShapeDtypeStruct((T, D), F32)]
        out_bytes = _nbytes((tm, D), F32)
    else:
        out_specs = [pl.BlockSpec((tm, D), row), pl.BlockSpec((tm, D), row)]
        out_shape = [jax.ShapeDtypeStruct((T, D), F32), jax.ShapeDtypeStruct((T, D), BF16)]
        out_bytes = _nbytes((tm, D), F32) + _nbytes((tm, D), BF16)
    blocks = (_nbytes((tm, D), F32) + _nbytes((tm, P), F32) + _nbytes((D, D), BF16) + _nbytes((P, D), BF16)
              + out_bytes)
    return pl.pallas_call(
        functools.partial(_ple_kernel, emit_h=not last),
        grid=(T // tm,),
        in_specs=[pl.BlockSpec((tm, D), row),
                  pl.BlockSpec((None, tm, P), lambda i: (layer, i, 0)),
                  pl.BlockSpec((None, 1, D), lambda i: (layer, 0, 0)),
                  pl.BlockSpec((1, D), lambda i: (0, 0)),
                  pl.BlockSpec((None, D, D), lambda i: (layer, 0, 0)),
                  pl.BlockSpec((None, P, D), lambda i: (layer, 0, 0))],
        out_specs=out_specs,
        out_shape=out_shape,
        compiler_params=_compiler_params(("parallel",), blocks, 3 * _nbytes((tm, D), F32)),
        name="ple_norm",
    )(h, p, g, g_next, wg, wp)


def _rope_tables(S):
    inv = ROPE_THETA ** (-jnp.arange(ROPE_HALF, dtype=F32) / ROPE_HALF)
    ang = jnp.arange(S, dtype=F32)[:, None] * inv[None, :]
    cos, sin = jnp.cos(ang), jnp.sin(ang)
    zeros = jnp.zeros((S, HEAD_DIM - ROPE_DIM), F32)
    half0 = jnp.zeros((S, ROPE_HALF), F32)
    cos_t = jnp.concatenate([cos, cos, jnp.ones((S, HEAD_DIM - ROPE_DIM), F32)], axis=1)
    sa_t = jnp.concatenate([half0, sin, zeros], axis=1)
    sb_t = jnp.concatenate([-sin, half0, zeros], axis=1)
    return cos_t, sa_t, sb_t


def kernel(x, p, g_mix, w_in, b_f, w_gate, b_gate, w_br_a, w_br_b, w_o, g_mlp, w_up, w_down, g_ple, w_ple,
           w_ple_gate, g_final):
    B, S, D = x.shape
    depth = p.shape[0]
    T = B * S
    assert all(w // d == BLOCK and S % (d * BLOCK) == 0 for w, d in DIL_GROUPS)
    assert w_in.shape[2] == 3 * ATTN_WIDTH + N_FOX_HEADS and (3 * ATTN_WIDTH) % D == 0

    tm_in, tn_in = min(1024, S), 1024
    tm_row = 256
    tm_mlp, tf_mlp = 1024, 512
    tm_ple = 512
    tq_fox = tk_fox = 512

    A = ATTN_WIDTH
    w_proj = jnp.concatenate(
        [w_in[:, :, 0:DIL_WIDTH], w_in[:, :, A:A + DIL_WIDTH], w_in[:, :, 2 * A:2 * A + DIL_WIDTH],
         w_in[:, :, DIL_WIDTH:A], w_in[:, :, A + DIL_WIDTH:2 * A], w_in[:, :, 2 * A + DIL_WIDTH:3 * A],
         w_gate], axis=2).astype(BF16)
    b_proj = jnp.concatenate([jnp.zeros((depth, 3 * A), F32), b_gate], axis=1)[:, None, :]
    w_f = jnp.pad(w_in[:, :, 3 * A:], ((0, 0), (0, 0), (0, LANES - N_FOX_HEADS))).astype(BF16)
    b_fp = jnp.pad(b_f, ((0, 0), (0, LANES - N_FOX_HEADS)))[:, None, :]
    wa, wb, wo = w_br_a.astype(BF16), w_br_b.astype(BF16), w_o.astype(BF16)
    wu, wd = w_up.astype(BF16), w_down.astype(BF16)
    wpg, wp = w_ple_gate.astype(BF16), w_ple.astype(BF16)
    g_mlp3, g_ple3 = g_mlp[:, None, :], g_ple[:, None, :]
    p3 = p.reshape(depth, T, p.shape[3])
    cos_t, sa_t, sb_t = _rope_tables(S)

    h = x.reshape(T, D)
    u = _rms_call(h, g_mix[0][None, :], tm_ple)
    for layer in range(depth):
        z = _in_proj_call(u, w_proj, b_proj, cos_t, sa_t, sb_t, layer, S, tm_in, tn_in)
        ccol, crow = _decay_call(u, w_f, b_fp, layer, B, S)
        dil = [_dil_call(z, g, d, B, S) for g, (_, d) in enumerate(DIL_GROUPS)]
        yb = _fox_call(z, ccol, crow, B, S, tq_fox, tk_fox)
        h = _merge_call([o for o, _ in dil], [l for _, l in dil], yb, z, h, wa, wb, wo, layer, tm_row)
        h = _mlp_call(h, g_mlp3, wu, wd, layer, tm_mlp, tf_mlp)
        last = layer == depth - 1
        g_next = g_final[None, :] if last else g_mix[layer + 1][None, :]
        outs = _ple_call(h, p3, g_ple3, g_next, wpg, wp, layer, tm_ple, last)
        if last:
            return outs[0].reshape(B, S, D)
        h, u = outs
```

```python
import functools

import jax
import jax.numpy as jnp
from jax import lax
from jax.experimental import pallas as pl
from jax.experimental.pallas import tpu as pltpu

HEAD_DIM = 128
BLOCK = 128
DIL_GROUPS = ((128, 1), (512, 4), (2048, 16))
HEADS_PER_DIL = 4
N_DIL_HEADS = HEADS_PER_DIL * len(DIL_GROUPS)
N_FOX_HEADS = 4
N_HEADS = N_DIL_HEADS + N_FOX_HEADS
ATTN_WIDTH = N_HEADS * HEAD_DIM
GROUP_WIDTH = HEADS_PER_DIL * HEAD_DIM
DIL_WIDTH = N_DIL_HEADS * HEAD_DIM
ROPE_THETA = 500000.0
ROPE_DIM = HEAD_DIM // 4
ROPE_HALF = ROPE_DIM // 2
NORM_EPS = 1e-6
ATTN_SCALE = HEAD_DIM ** -0.5

LANES = 128
SUBLANES = 8
V7X_VMEM_LIMIT_CAP = 56 * 1024 * 1024

F32 = jnp.float32
BF16 = jnp.bfloat16

PERM_DILATIONS = tuple(d for _, d in DIL_GROUPS if d > 1)


def _nbytes(shape, dtype):
    n = 1
    for s in shape:
        n *= s
    return n * jnp.dtype(dtype).itemsize


def _compiler_params(semantics, pipelined, resident=0):
    need = 2 * pipelined + resident
    assert need <= V7X_VMEM_LIMIT_CAP, (need, V7X_VMEM_LIMIT_CAP)
    return pltpu.CompilerParams(dimension_semantics=semantics, vmem_limit_bytes=V7X_VMEM_LIMIT_CAP)


def _resident_spec(block_shape, index_map):
    return pl.BlockSpec(block_shape, index_map, pipeline_mode=pl.Buffered(1))


def _rms(x, g):
    return x * lax.rsqrt(jnp.mean(x * x, axis=-1, keepdims=True) + NORM_EPS) * g


def _sigmoid(x):
    return 0.5 * jnp.tanh(0.5 * x) + 0.5


def _rope(x, c, s):
    return x * c + pltpu.roll(x, HEAD_DIM // 2, 1) * s


def _emit_u(u, sub, scr_ref, u_ref, perm_refs):
    rows, n_chunks = u.shape[0], u.shape[1] // LANES
    row0 = sub * rows
    u_ref[row0:row0 + rows, :] = u.astype(u_ref.dtype)
    for c in range(n_chunks):
        scr_ref[sub * n_chunks + c] = u[:, c * LANES:(c + 1) * LANES]
    for ref in perm_refs:
        d = ref.shape[1]
        steps, m0 = rows // d, row0 // d
        for r in range(d):
            for c in range(n_chunks):
                ref[0, r, m0:m0 + steps, c * LANES:(c + 1) * LANES] = (
                    scr_ref[sub * n_chunks + c, pl.ds(r, steps, stride=d), :].astype(ref.dtype))


def _u_out_specs(T, D, B, S, tm):
    per_seq = S // tm
    specs = [pl.BlockSpec((tm, D), lambda i: (i, 0))]
    shapes = [jax.ShapeDtypeStruct((T, D), BF16)]
    for d in PERM_DILATIONS:
        specs.append(pl.BlockSpec((1, d, tm // d, D), lambda i: (i // per_seq, 0, i % per_seq, 0)))
        shapes.append(jax.ShapeDtypeStruct((B, d, S // d, D), BF16))
    return specs, shapes


def _rms_kernel(x_ref, g_ref, *refs):
    outs, scr_ref = refs[:-1], refs[-1]
    _emit_u(_rms(x_ref[...], g_ref[...]), 0, scr_ref, outs[0], outs[1:])


def _rms_call(x, g, B, S, tm):
    T, D = x.shape
    specs, shapes = _u_out_specs(T, D, B, S, tm)
    blocks = _nbytes((tm, D), F32) + (1 + len(PERM_DILATIONS)) * _nbytes((tm, D), BF16)
    return pl.pallas_call(
        _rms_kernel,
        grid=(T // tm,),
        in_specs=[pl.BlockSpec((tm, D), lambda i: (i, 0)),
                  pl.BlockSpec((1, D), lambda i: (0, 0))],
        out_specs=specs,
        out_shape=shapes,
        scratch_shapes=[pltpu.VMEM((D // LANES, tm, LANES), F32)],
        compiler_params=_compiler_params(("parallel",), blocks, 2 * _nbytes((tm, D), F32)),
        name="rms_in",
    )(x, g)


def _in_proj_kernel(u_ref, w_ref, b_ref, cos_ref, sin_ref, o_ref, *, gate_tiles, rope_tiles, chunk):
    j = pl.program_id(1)
    tn = o_ref.shape[1]

    def chunks():
        for c in range(tn // chunk):
            cols = slice(c * chunk, (c + 1) * chunk)
            yield cols, jnp.dot(u_ref[...], w_ref[:, cols], preferred_element_type=F32)

    @pl.when(j < gate_tiles)
    def _gate():
        for cols, acc in chunks():
            o_ref[:, cols] = _sigmoid(acc + b_ref[:, cols]).astype(o_ref.dtype)

    @pl.when((j >= gate_tiles) & (j < gate_tiles + rope_tiles))
    def _rotary():
        c, s = cos_ref[...], sin_ref[...]
        for cols, acc in chunks():
            for hh in range(chunk // HEAD_DIM):
                x = acc[:, hh * HEAD_DIM:(hh + 1) * HEAD_DIM]
                lo = cols.start + hh * HEAD_DIM
                o_ref[:, lo:lo + HEAD_DIM] = _rope(x, c, s).astype(o_ref.dtype)

    @pl.when(j >= gate_tiles + rope_tiles)
    def _plain():
        for cols, acc in chunks():
            o_ref[:, cols] = acc.astype(o_ref.dtype)


def _in_proj_call(u, w, b, cos_t, sin_t, layer, S, D_gate, tm, tn, chunk):
    T, D = u.shape
    N = w.shape[2]
    gate_tiles = D_gate // tn
    rope_tiles = (2 * GROUP_WIDTH) // tn
    pos_blocks = S // tm
    blocks = (_nbytes((tm, D), BF16) + _nbytes((D, tn), BF16) + _nbytes((tm, tn), BF16)
              + 2 * _nbytes((tm, LANES), F32))
    return pl.pallas_call(
        functools.partial(_in_proj_kernel, gate_tiles=gate_tiles, rope_tiles=rope_tiles, chunk=chunk),
        grid=(T // tm, N // tn),
        in_specs=[pl.BlockSpec((tm, D), lambda i, j: (i, 0)),
                  pl.BlockSpec((None, D, tn), lambda i, j: (layer, 0, j)),
                  pl.BlockSpec((None, 1, tn), lambda i, j: (layer, 0, jnp.minimum(j, gate_tiles - 1))),
                  pl.BlockSpec((tm, LANES), lambda i, j: (i % pos_blocks, 0)),
                  pl.BlockSpec((tm, LANES), lambda i, j: (i % pos_blocks, 0))],
        out_specs=pl.BlockSpec((tm, tn), lambda i, j: (i, j)),
        out_shape=jax.ShapeDtypeStruct((T, N), BF16),
        compiler_params=_compiler_params(("parallel", "arbitrary"), blocks, 3 * _nbytes((tm, chunk), F32)),
        name="in_proj",
    )(u, w, b, cos_t, sin_t)


def _in_proj_dil_kernel(u_ref, w_ref, cos_ref, sin_ref, o_ref):
    c, s = cos_ref[...], sin_ref[...]
    for g in range(o_ref.shape[1] // GROUP_WIDTH):
        cols = slice(g * GROUP_WIDTH, (g + 1) * GROUP_WIDTH)
        acc = jnp.dot(u_ref[...], w_ref[:, cols], preferred_element_type=F32)
        if g < 2:
            for hh in range(HEADS_PER_DIL):
                x = acc[:, hh * HEAD_DIM:(hh + 1) * HEAD_DIM]
                lo = cols.start + hh * HEAD_DIM
                o_ref[:, lo:lo + HEAD_DIM] = _rope(x, c, s).astype(o_ref.dtype)
        else:
            o_ref[:, cols] = acc.astype(o_ref.dtype)


def _in_proj_dil_call(u_perm, w, cos_p, sin_p, layer, S, tm, name):
    T, D = u_perm.shape
    N = w.shape[2]
    pos_blocks = S // tm
    blocks = _nbytes((tm, D), BF16) + _nbytes((tm, N), BF16) + 2 * _nbytes((tm, LANES), F32)
    return pl.pallas_call(
        _in_proj_dil_kernel,
        grid=(T // tm,),
        in_specs=[pl.BlockSpec((tm, D), lambda i: (i, 0)),
                  _resident_spec((None, D, N), lambda i: (layer, 0, 0)),
                  pl.BlockSpec((tm, LANES), lambda i: (i % pos_blocks, 0)),
                  pl.BlockSpec((tm, LANES), lambda i: (i % pos_blocks, 0))],
        out_specs=pl.BlockSpec((tm, N), lambda i: (i, 0)),
        out_shape=jax.ShapeDtypeStruct((T, N), BF16),
        compiler_params=_compiler_params(("parallel",), blocks,
                                         _nbytes((D, N), BF16) + 3 * _nbytes((tm, GROUP_WIDTH), F32)),
        name=name,
    )(u_perm, w, cos_p, sin_p)


def _decay_kernel(u_ref, w_ref, b_ref, ccol_ref, crow_ref):
    fl = jnp.dot(u_ref[...], w_ref[...], preferred_element_type=F32) + b_ref[...]
    ls = jnp.minimum(fl, 0.0) - jnp.log1p(jnp.exp(-jnp.abs(fl)))
    ri = lax.broadcasted_iota(jnp.int32, (BLOCK, BLOCK), 0)
    ci = lax.broadcasted_iota(jnp.int32, (BLOCK, BLOCK), 1)
    tri = (ci <= ri).astype(F32)
    carry = jnp.zeros((1, LANES), F32)
    for n in range(fl.shape[0] // BLOCK):
        rows = slice(n * BLOCK, (n + 1) * BLOCK)
        blk = jnp.dot(tri, ls[rows], precision=lax.Precision.HIGHEST, preferred_element_type=F32) + carry
        ccol_ref[0, rows, :] = blk
        crow_ref[0, :, rows] = blk.T[0:SUBLANES, :]
        carry = blk[BLOCK - 1:BLOCK, :]


def _decay_call(u, w, b, layer, B, S):
    T, D = u.shape
    blocks = _nbytes((S, D), BF16) + _nbytes((S, LANES), F32) + _nbytes((SUBLANES, S), F32)
    return pl.pallas_call(
        _decay_kernel,
        grid=(B,),
        in_specs=[pl.BlockSpec((S, D), lambda b_: (b_, 0)),
                  _resident_spec((None, D, LANES), lambda b_: (layer, 0, 0)),
                  _resident_spec((None, 1, LANES), lambda b_: (layer, 0, 0))],
        out_specs=[pl.BlockSpec((1, S, LANES), lambda b_: (b_, 0, 0)),
                   pl.BlockSpec((1, SUBLANES, S), lambda b_: (b_, 0, 0))],
        out_shape=[jax.ShapeDtypeStruct((B, S, LANES), F32),
                   jax.ShapeDtypeStruct((B, SUBLANES, S), F32)],
        compiler_params=_compiler_params(("parallel",), blocks,
                                         _nbytes((D, LANES), BF16) + 4 * _nbytes((S, LANES), F32)),
        name="fox_decay",
    )(u, w, b)


def _dil_kernel(q_ref, k_ref, v_ref, o_ref, lse_ref, *, dilation, nb, diag_group, band_group):
    qi = lax.broadcasted_iota(jnp.int32, (BLOCK, 2 * BLOCK), 0)
    ki = lax.broadcasted_iota(jnp.int32, (BLOCK, 2 * BLOCK), 1)
    band = (ki >= qi) & (ki <= qi + BLOCK)
    qd = lax.broadcasted_iota(jnp.int32, (BLOCK, BLOCK), 0)
    kd = lax.broadcasted_iota(jnp.int32, (BLOCK, BLOCK), 1)
    diag = kd <= qd

    def run(units, klen, mask):
        jobs = [(r, q0, k0, h) for (r, q0, k0) in units for h in range(HEADS_PER_DIL)]
        scores = []
        for r, q0, k0, h in jobs:
            cs = pl.ds(h * HEAD_DIM, HEAD_DIM)
            s = lax.dot_general(q_ref[0, r, pl.ds(q0, BLOCK), cs], k_ref[0, r, pl.ds(k0, klen), cs],
                                (((1,), (1,)), ((), ())), preferred_element_type=F32) * ATTN_SCALE
            scores.append(jnp.where(mask, s, -jnp.inf))
        probs = []
        for s in scores:
            m = jnp.max(s, axis=-1, keepdims=True)
            p = jnp.exp(s - m)
            l = jnp.sum(p, axis=-1, keepdims=True)
            probs.append((p.astype(BF16), 1.0 / l, m + jnp.log(l)))
        for (r, q0, k0, h), (p, inv_l, lse) in zip(jobs, probs):
            v = v_ref[0, r, pl.ds(k0, klen), pl.ds(h * HEAD_DIM, HEAD_DIM)]
            o = jnp.dot(p, v, preferred_element_type=F32) * inv_l
            out_rows = pl.ds(q0 * dilation + r, BLOCK, stride=dilation) if dilation > 1 else pl.ds(q0, BLOCK)
            o_ref[0, h, out_rows, :] = o
            lse_ref[0, h, out_rows, :] = jnp.broadcast_to(lse, (BLOCK, HEAD_DIM))

    def loop(trips, body):
        if trips == 1:
            body(0)
        elif trips > 1:
            lax.fori_loop(0, trips, lambda t, c: (body(t), c)[1], 0)

    loop(dilation // diag_group,
         lambda g: run([(g * diag_group + t, 0, 0) for t in range(diag_group)], BLOCK, diag))

    def band_blocks(r):
        def body(g):
            units = []
            for t in range(band_group):
                q0 = (1 + g * band_group + t) * BLOCK
                k0 = q0 - BLOCK
                if not isinstance(q0, int):
                    q0, k0 = pl.multiple_of(q0, BLOCK), pl.multiple_of(k0, BLOCK)
                units.append((r, q0, k0))
            run(units, 2 * BLOCK, band)

        loop((nb - 1) // band_group, body)

    loop(dilation, band_blocks)


def _dil_call(z, col0, dilation, B, S):
    T, N = z.shape
    L = S // dilation
    nb = L // BLOCK
    zv = z.reshape(B, dilation, L, N)
    cb = col0 // GROUP_WIDTH

    def spec(c):
        return pl.BlockSpec((1, dilation, L, GROUP_WIDTH), lambda b_: (b_, 0, 0, cb + c))

    out_spec = pl.BlockSpec((1, HEADS_PER_DIL, S, HEAD_DIM), lambda b_: (b_, 0, 0, 0))
    out_sds = jax.ShapeDtypeStruct((B, HEADS_PER_DIL, S, HEAD_DIM), F32)
    blocks = 3 * _nbytes((S, GROUP_WIDTH), BF16) + 2 * _nbytes((S, GROUP_WIDTH), F32)
    return pl.pallas_call(
        functools.partial(_dil_kernel, dilation=dilation, nb=nb, diag_group=min(dilation, 4),
                          band_group=3 if (nb - 1) % 3 == 0 else 1),
        grid=(B,),
        in_specs=[spec(0), spec(1), spec(2)],
        out_specs=[out_spec, out_spec],
        out_shape=[out_sds, out_sds],
        compiler_params=_compiler_params(("parallel",), blocks),
        name=f"dilated_d{dilation}",
    )(zv, zv, zv)


def _fox_kernel(q_ref, k_ref, v_ref, cq_ref, ck_ref, o_ref, *, tile):
    i = pl.program_id(1)
    qd = lax.broadcasted_iota(jnp.int32, (tile, tile), 0)
    kd = lax.broadcasted_iota(jnp.int32, (tile, tile), 1)
    causal = kd <= qd

    def step(t, carry, masked):
        k0 = pl.multiple_of(t * tile, tile)
        heads = [pl.ds(h * HEAD_DIM, HEAD_DIM) for h in range(N_FOX_HEADS)]
        scores = []
        for h, cs in enumerate(heads):
            s = lax.dot_general(q_ref[0, :, cs], k_ref[0, pl.ds(k0, tile), cs], (((1,), (1,)), ((), ())),
                                preferred_element_type=F32)
            s = s * ATTN_SCALE + (cq_ref[0, :, h:h + 1] - ck_ref[0, h:h + 1, pl.ds(k0, tile)])
            scores.append(jnp.where(causal, s, -jnp.inf) if masked else s)
        probs = []
        for s, (m, l, _) in zip(scores, carry):
            m_new = jnp.maximum(m, jnp.max(s, axis=-1, keepdims=True))
            alpha = jnp.exp(m - m_new)
            p = jnp.exp(s - m_new)
            probs.append((p.astype(BF16), alpha, m_new, alpha * l + jnp.sum(p, axis=-1, keepdims=True)))
        new = []
        for cs, (p, alpha, m_new, l), (_, _, acc) in zip(heads, probs, carry):
            acc = alpha * acc + jnp.dot(p, v_ref[0, pl.ds(k0, tile), cs], preferred_element_type=F32)
            new.append((m_new, l, acc))
        return tuple(new)

    init = tuple((jnp.full((tile, 1), -jnp.inf, F32), jnp.zeros((tile, 1), F32), jnp.zeros((tile, HEAD_DIM), F32))
                 for _ in range(N_FOX_HEADS))
    carry = lax.fori_loop(0, i, lambda t, c: step(t, c, False), init)
    carry = step(i, carry, True)
    for h in range(N_FOX_HEADS):
        _, l, acc = carry[h]
        o_ref[0, :, pl.ds(h * HEAD_DIM, HEAD_DIM)] = (acc * (1.0 / l)).astype(o_ref.dtype)


def _fox_call(z, col0, ccol, crow, B, S, tile):
    T, N = z.shape
    zv = z.reshape(B, S, N)
    base = col0 // GROUP_WIDTH
    blocks = (_nbytes((tile, GROUP_WIDTH), BF16) * 2 + 2 * _nbytes((S, GROUP_WIDTH), BF16)
              + _nbytes((tile, LANES), F32) + _nbytes((SUBLANES, S), F32))
    out = pl.pallas_call(
        functools.partial(_fox_kernel, tile=tile),
        grid=(B, S // tile),
        in_specs=[pl.BlockSpec((1, tile, GROUP_WIDTH), lambda b_, i: (b_, i, base)),
                  pl.BlockSpec((1, S, GROUP_WIDTH), lambda b_, i: (b_, 0, base + 1)),
                  pl.BlockSpec((1, S, GROUP_WIDTH), lambda b_, i: (b_, 0, base + 2)),
                  pl.BlockSpec((1, tile, LANES), lambda b_, i: (b_, i, 0)),
                  pl.BlockSpec((1, SUBLANES, S), lambda b_, i: (b_, 0, 0))],
        out_specs=pl.BlockSpec((1, tile, GROUP_WIDTH), lambda b_, i: (b_, i, 0)),
        out_shape=jax.ShapeDtypeStruct((B, S, GROUP_WIDTH), BF16),
        compiler_params=_compiler_params(("parallel", "arbitrary"), blocks, 16 * _nbytes((tile, tile), F32)),
        name="fox_attention",
    )(zv, zv, zv, ccol, crow)
    return out.reshape(T, GROUP_WIDTH)


def _merge_kernel(o0_ref, o1_ref, o2_ref, l0_ref, l1_ref, l2_ref, yb_ref, ga_ref, gb_ref, h_ref,
                  wa_ref, wb_ref, wo_ref, h1_ref):
    heads = []
    for hd in range(HEADS_PER_DIL):
        l0, l1, l2 = l0_ref[0, hd], l1_ref[0, hd], l2_ref[0, hd]
        mx = jnp.maximum(jnp.maximum(l0, l1), l2)
        e0, e1, e2 = jnp.exp(l0 - mx), jnp.exp(l1 - mx), jnp.exp(l2 - mx)
        inv = 1.0 / (e0 + e1 + e2)
        ya_h = (e0 * inv) * o0_ref[0, hd] + (e1 * inv) * o1_ref[0, hd] + (e2 * inv) * o2_ref[0, hd]
        heads.append(ya_h.astype(BF16))
    ya = jnp.concatenate(heads, axis=1)
    ya_p = jnp.dot(ya, wa_ref[...], preferred_element_type=F32)
    yb_p = jnp.dot(yb_ref[...], wb_ref[...], preferred_element_type=F32)
    merged = ga_ref[...].astype(F32) * ya_p + gb_ref[...].astype(F32) * yb_p
    h1_ref[...] = h_ref[...] + jnp.dot(merged.astype(BF16), wo_ref[...], preferred_element_type=F32)


def _merge_call(os_, ls_, yb, z, h, wa, wb, wo, layer, S, tm):
    T, D = h.shape
    row = lambda i: (i, 0)
    per_seq = S // tm
    hw = pl.BlockSpec((1, HEADS_PER_DIL, tm, HEAD_DIM), lambda i: (i // per_seq, 0, i % per_seq, 0))
    blocks = (6 * _nbytes((tm, GROUP_WIDTH), F32) + _nbytes((tm, GROUP_WIDTH), BF16) + 2 * _nbytes((tm, D), BF16)
              + 2 * _nbytes((tm, D), F32))
    weights = 2 * _nbytes((GROUP_WIDTH, D), BF16) + _nbytes((D, D), BF16)
    return pl.pallas_call(
        _merge_kernel,
        grid=(T // tm,),
        in_specs=[hw, hw, hw, hw, hw, hw, pl.BlockSpec((tm, GROUP_WIDTH), row),
                  pl.BlockSpec((tm, D), lambda i: (i, 0)),
                  pl.BlockSpec((tm, D), lambda i: (i, 1)),
                  pl.BlockSpec((tm, D), row),
                  _resident_spec((None, GROUP_WIDTH, D), lambda i: (layer, 0, 0)),
                  _resident_spec((None, GROUP_WIDTH, D), lambda i: (layer, 0, 0)),
                  _resident_spec((None, D, D), lambda i: (layer, 0, 0))],
        out_specs=pl.BlockSpec((tm, D), row),
        out_shape=jax.ShapeDtypeStruct((T, D), F32),
        compiler_params=_compiler_params(("parallel",), blocks, weights + 3 * _nbytes((tm, D), F32)),
        name="merge_wo",
    )(*os_, *ls_, yb, z, z, h, wa, wb, wo)


def _mlp_kernel(h_ref, g_ref, wu_ref, wd_ref, o_ref, m_ref):
    j = pl.program_id(1)

    @pl.when(j == 0)
    def _init():
        h = h_ref[...]
        m_ref[...] = _rms(h, g_ref[...]).astype(m_ref.dtype)
        o_ref[...] = h

    a = jnp.dot(m_ref[...], wu_ref[...], preferred_element_type=F32)
    a = jnp.square(jnp.maximum(a, 0.0)).astype(BF16)
    o_ref[...] += jnp.dot(a, wd_ref[...], preferred_element_type=F32)


def _mlp_call(h, g, wu, wd, layer, tm, tf):
    T, D = h.shape
    FF = wu.shape[2]
    blocks = 2 * _nbytes((tm, D), F32) + 2 * _nbytes((D, tf), BF16)
    resident = _nbytes((tm, D), BF16) + _nbytes((tm, tf), F32) + _nbytes((tm, tf), BF16)
    return pl.pallas_call(
        _mlp_kernel,
        grid=(T // tm, FF // tf),
        in_specs=[pl.BlockSpec((tm, D), lambda i, j: (i, 0)),
                  pl.BlockSpec((None, 1, D), lambda i, j: (layer, 0, 0)),
                  pl.BlockSpec((None, D, tf), lambda i, j: (layer, 0, j)),
                  pl.BlockSpec((None, tf, D), lambda i, j: (layer, j, 0))],
        out_specs=pl.BlockSpec((tm, D), lambda i, j: (i, 0)),
        out_shape=jax.ShapeDtypeStruct((T, D), F32),
        scratch_shapes=[pltpu.VMEM((tm, D), BF16)],
        compiler_params=_compiler_params(("parallel", "arbitrary"), blocks, resident),
        name="mlp",
    )(h, g, wu, wd)


def _ple_kernel(h_ref, p_ref, g_ref, gn_ref, wg_ref, wp_ref, *refs, last, n_sub):
    rows = h_ref.shape[0] // n_sub
    parts = []
    for t in range(n_sub):
        rs = slice(t * rows, (t + 1) * rows)
        n = _rms(h_ref[rs, :], g_ref[...]).astype(BF16)
        parts.append((jnp.dot(n, wg_ref[...], preferred_element_type=F32),
                      jnp.dot(p_ref[rs, :].astype(BF16), wp_ref[...], preferred_element_type=F32)))
    for t, (gate_logit, pp) in enumerate(parts):
        rs = slice(t * rows, (t + 1) * rows)
        h3 = h_ref[rs, :] + _sigmoid(gate_logit) * pp
        if last:
            refs[0][rs, :] = _rms(h3, gn_ref[...])
        else:
            refs[0][rs, :] = h3
            _emit_u(_rms(h3, gn_ref[...]), t, refs[-1], refs[1], refs[2:-1])


def _ple_call(h, p, g, g_next, wg, wp, layer, B, S, tm, n_sub, last):
    T, D = h.shape
    P = p.shape[2]
    row = lambda i: (i, 0)
    if last:
        out_specs = [pl.BlockSpec((tm, D), row)]
        out_shape = [jax.ShapeDtypeStruct((T, D), F32)]
        out_bytes = _nbytes((tm, D), F32)
        scratch = []
    else:
        u_specs, u_shapes = _u_out_specs(T, D, B, S, tm)
        out_specs = [pl.BlockSpec((tm, D), row)] + u_specs
        out_shape = [jax.ShapeDtypeStruct((T, D), F32)] + u_shapes
        out_bytes = _nbytes((tm, D), F32) + (1 + len(PERM_DILATIONS)) * _nbytes((tm, D), BF16)
        scratch = [pltpu.VMEM((n_sub * (D // LANES), tm // n_sub, LANES), F32)]
    blocks = _nbytes((tm, D), F32) + _nbytes((tm, P), F32) + out_bytes
    weights = _nbytes((D, D), BF16) + _nbytes((P, D), BF16)
    return pl.pallas_call(
        functools.partial(_ple_kernel, last=last, n_sub=n_sub),
        grid=(T // tm,),
        in_specs=[pl.BlockSpec((tm, D), row),
                  pl.BlockSpec((None, tm, P), lambda i: (layer, i, 0)),
                  pl.BlockSpec((None, 1, D), lambda i: (layer, 0, 0)),
                  pl.BlockSpec((1, D), lambda i: (0, 0)),
                  _resident_spec((None, D, D), lambda i: (layer, 0, 0)),
                  _resident_spec((None, P, D), lambda i: (layer, 0, 0))],
        out_specs=out_specs,
        out_shape=out_shape,
        scratch_shapes=scratch,
        compiler_params=_compiler_params(("parallel",), blocks, weights + 4 * _nbytes((tm, D), F32)),
        name="ple_norm",
    )(h, p, g, g_next, wg, wp)


def _rope_tables(S):
    inv = ROPE_THETA ** (-jnp.arange(ROPE_HALF, dtype=F32) / ROPE_HALF)
    ang = jnp.arange(S, dtype=F32)[:, None] * inv[None, :]
    cos, sin = jnp.cos(ang), jnp.sin(ang)
    ones = jnp.ones((S, HEAD_DIM // 2 - ROPE_HALF), F32)
    zeros = jnp.zeros((S, HEAD_DIM // 2 - ROPE_HALF), F32)
    cos_t = jnp.concatenate([cos, ones, cos, ones], axis=1)
    sin_t = jnp.concatenate([-sin, zeros, sin, zeros], axis=1)
    return cos_t, sin_t


def _residue_major(t, d):
    S = t.shape[0]
    return t.reshape(S // d, d, t.shape[1]).transpose(1, 0, 2).reshape(S, t.shape[1])


def _rope_head_order(w):
    lead = w.shape[:-1]
    wh = w.reshape(*lead, -1, HEAD_DIM)
    mid = HEAD_DIM // 2 + ROPE_HALF
    wh = jnp.concatenate([wh[..., :ROPE_HALF], wh[..., ROPE_DIM:mid], wh[..., ROPE_HALF:ROPE_DIM], wh[..., mid:]],
                         axis=-1)
    return wh.reshape(*lead, -1)


def kernel(x, p, g_mix, w_in, b_f, w_gate, b_gate, w_br_a, w_br_b, w_o, g_mlp, w_up, w_down, g_ple, w_ple,
           w_ple_gate, g_final):
    B, S, D = x.shape
    depth = p.shape[0]
    T = B * S
    A, G = ATTN_WIDTH, GROUP_WIDTH
    assert all(w // d == BLOCK and S % (d * BLOCK) == 0 for w, d in DIL_GROUPS) and DIL_GROUPS[0][1] == 1
    assert w_in.shape[2] == 3 * A + N_FOX_HEADS and w_gate.shape[2] == 2 * D

    tm_in, tn_in, chunk_in = min(1024, S), 1024, 512
    tm_row = 256
    tm_ple, sub_ple = 512, 2
    tm_mlp, tf_mlp = 1024, 512
    fox_tile = 256

    wq = _rope_head_order(w_in[:, :, 0:DIL_WIDTH])
    wk = _rope_head_order(w_in[:, :, A:A + DIL_WIDTH])
    wv = w_in[:, :, 2 * A:2 * A + DIL_WIDTH]
    w_main = jnp.concatenate(
        [w_gate, wq[:, :, :G], wk[:, :, :G], wv[:, :, :G],
         w_in[:, :, DIL_WIDTH:A], w_in[:, :, A + DIL_WIDTH:2 * A], w_in[:, :, 2 * A + DIL_WIDTH:3 * A]],
        axis=2).astype(BF16)
    w_dil = [jnp.concatenate([wq[:, :, g * G:(g + 1) * G], wk[:, :, g * G:(g + 1) * G], wv[:, :, g * G:(g + 1) * G]],
                             axis=2).astype(BF16) for g in range(1, len(DIL_GROUPS))]
    b_main = b_gate[:, None, :]
    w_f = jnp.pad(w_in[:, :, 3 * A:], ((0, 0), (0, 0), (0, LANES - N_FOX_HEADS))).astype(BF16)
    b_fp = jnp.pad(b_f, ((0, 0), (0, LANES - N_FOX_HEADS)))[:, None, :]
    wa, wb, wo = w_br_a.astype(BF16), w_br_b.astype(BF16), w_o.astype(BF16)
    wu, wd = w_up.astype(BF16), w_down.astype(BF16)
    wpg, wp = w_ple_gate.astype(BF16), w_ple.astype(BF16)
    g_mlp3, g_ple3 = g_mlp[:, None, :], g_ple[:, None, :]
    p3 = p.reshape(depth, T, p.shape[3])
    cos_t, sin_t = _rope_tables(S)
    tables_p = [(_residue_major(cos_t, d), _residue_major(sin_t, d)) for d in PERM_DILATIONS]
    qkv0_col = 2 * D
    fox_col = qkv0_col + 3 * G

    h = x.reshape(T, D)
    u, *u_perm = _rms_call(h, g_mix[0][None, :], B, S, tm_row)
    for layer in range(depth):
        z = _in_proj_call(u, w_main, b_main, cos_t, sin_t, layer, S, 2 * D, tm_in, tn_in, chunk_in)
        ccol, crow = _decay_call(u, w_f, b_fp, layer, B, S)
        dil = [_dil_call(z, qkv0_col, 1, B, S)]
        for up, w_g, (cos_p, sin_p), d in zip(u_perm, w_dil, tables_p, PERM_DILATIONS):
            z_g = _in_proj_dil_call(up.reshape(T, D), w_g, cos_p, sin_p, layer, S, tm_in, f"in_proj_d{d}")
            dil.append(_dil_call(z_g, 0, d, B, S))
        yb = _fox_call(z, fox_col, ccol, crow, B, S, fox_tile)
        h = _merge_call([o for o, _ in dil], [l for _, l in dil], yb, z, h, wa, wb, wo, layer, S, tm_row)
        h = _mlp_call(h, g_mlp3, wu, wd, layer, tm_mlp, tf_mlp)
        last = layer == depth - 1
        g_next = g_final[None, :] if last else g_mix[layer + 1][None, :]
        outs = _ple_call(h, p3, g_ple3, g_next, wpg, wp, layer, B, S, tm_ple, sub_ple, last)
        if last:
            return outs[0].reshape(B, S, D)
        h, u, *u_perm = outs
```

```python
import functools

import jax
import jax.numpy as jnp
from jax import lax
from jax.experimental import pallas as pl
from jax.experimental.pallas import tpu as pltpu

HEAD_DIM = 128
BLOCK = 128
DIL_GROUPS = ((128, 1), (512, 4), (2048, 16))
HEADS_PER_DIL = 4
N_DIL_HEADS = HEADS_PER_DIL * len(DIL_GROUPS)
N_FOX_HEADS = 4
N_HEADS = N_DIL_HEADS + N_FOX_HEADS
ATTN_WIDTH = N_HEADS * HEAD_DIM
GROUP_WIDTH = HEADS_PER_DIL * HEAD_DIM
N_GROUPS = ATTN_WIDTH // GROUP_WIDTH
ROPE_THETA = 500000.0
ROPE_DIM = HEAD_DIM // 4
ROPE_HALF = ROPE_DIM // 2
NORM_EPS = 1e-6
ATTN_SCALE = HEAD_DIM ** -0.5

LANES = 128
SUBLANES = 8
V7X_VMEM_LIMIT_CAP = 58 * 1024 * 1024

F32 = jnp.float32
BF16 = jnp.bfloat16

PERM_DILATIONS = tuple(d for _, d in DIL_GROUPS if d > 1)


def _nbytes(shape, dtype):
    n = 1
    for s in shape:
        n *= s
    return n * jnp.dtype(dtype).itemsize


def _compiler_params(semantics, pipelined, resident=0):
    need = 2 * pipelined + resident
    assert need <= V7X_VMEM_LIMIT_CAP, (need, V7X_VMEM_LIMIT_CAP)
    return pltpu.CompilerParams(dimension_semantics=semantics, vmem_limit_bytes=V7X_VMEM_LIMIT_CAP)


def _resident_spec(block_shape, index_map):
    return pl.BlockSpec(block_shape, index_map, pipeline_mode=pl.Buffered(1))


def _rms(x, g):
    return x * lax.rsqrt(jnp.mean(x * x, axis=-1, keepdims=True) + NORM_EPS) * g


def _sigmoid(x):
    return 0.5 * jnp.tanh(0.5 * x) + 0.5


def _rope(x, c, s):
    return x * c + pltpu.roll(x, HEAD_DIM // 2, 1) * s


def _emit_u(u, sub, scr_ref, u_ref, perm_refs):
    rows, n_chunks = u.shape[0], u.shape[1] // LANES
    row0 = sub * rows
    u_ref[row0:row0 + rows, :] = u.astype(u_ref.dtype)
    n_sub = scr_ref.shape[0] // (len(perm_refs) * n_chunks)

    def slot(level, c):
        return (level * n_sub + sub) * n_chunks + c

    for c in range(n_chunks):
        scr_ref[slot(0, c)] = u[:, c * LANES:(c + 1) * LANES]
    d_prev = 1
    for level, ref in enumerate(perm_refs):
        d = ref.shape[1]
        f, steps = d // d_prev, rows // d
        for c in range(n_chunks):
            lanes = slice(c * LANES, (c + 1) * LANES)
            for b in range(f):
                part = scr_ref[slot(level, c), pl.ds(b, rows // f, stride=f), :]
                if level + 1 < len(perm_refs):
                    scr_ref[slot(level + 1, c), b * (rows // f):(b + 1) * (rows // f), :] = part
                for rr in range(d // f):
                    ref[0, b * (d // f) + rr, row0 // d:row0 // d + steps, lanes] = (
                        part[rr * steps:(rr + 1) * steps, :].astype(ref.dtype))
        d_prev = d


def _u_scratch(D, tm, n_sub):
    return pltpu.VMEM((len(PERM_DILATIONS) * n_sub * (D // LANES), tm // n_sub, LANES), F32)


def _u_out_specs(T, D, B, S, tm):
    per_seq = S // tm
    specs = [pl.BlockSpec((tm, D), lambda i: (i, 0))]
    shapes = [jax.ShapeDtypeStruct((T, D), BF16)]
    for d in PERM_DILATIONS:
        specs.append(pl.BlockSpec((1, d, tm // d, D), lambda i: (i // per_seq, 0, i % per_seq, 0)))
        shapes.append(jax.ShapeDtypeStruct((B, d, S // d, D), BF16))
    return specs, shapes


def _qkv_prep_kernel(w_ref, o_ref, *, rope_blocks):
    s = pl.program_id(1)

    @pl.when(s < rope_blocks)
    def _rotary_heads():
        lane = lax.broadcasted_iota(jnp.int32, (w_ref.shape[1], HEAD_DIM), 1)
        mid = HEAD_DIM // 2
        for hh in range(o_ref.shape[2] // HEAD_DIM):
            cols = slice(hh * HEAD_DIM, (hh + 1) * HEAD_DIM)
            x = w_ref[0, :, cols]
            up = pltpu.roll(x, HEAD_DIM - ROPE_HALF, 1)
            down = pltpu.roll(x, mid - ROPE_HALF, 1)
            y = jnp.where(lane < ROPE_HALF, x,
                          jnp.where(lane < mid, up, jnp.where(lane < mid + ROPE_HALF, down, x)))
            o_ref[0, :, cols] = y.astype(o_ref.dtype)

    @pl.when(s >= rope_blocks)
    def _plain():
        o_ref[...] = w_ref[...].astype(o_ref.dtype)


def _qkv_prep_call(w_in, src_block, n_blocks, rope_blocks, name):
    depth, D, _ = w_in.shape
    G = GROUP_WIDTH
    return pl.pallas_call(
        functools.partial(_qkv_prep_kernel, rope_blocks=rope_blocks),
        grid=(depth, n_blocks),
        in_specs=[pl.BlockSpec((1, D, G), lambda l, s: (l, 0, src_block(s)))],
        out_specs=pl.BlockSpec((1, D, G), lambda l, s: (l, 0, s)),
        out_shape=jax.ShapeDtypeStruct((depth, D, n_blocks * G), BF16),
        compiler_params=_compiler_params(("parallel", "arbitrary"), _nbytes((D, G), F32) + _nbytes((D, G), BF16),
                                         3 * _nbytes((D, HEAD_DIM), F32)),
        name=name,
    )(w_in)


def _rms_kernel(x_ref, g_ref, *refs):
    outs, scr_ref = refs[:-1], refs[-1]
    _emit_u(_rms(x_ref[...], g_ref[...]), 0, scr_ref, outs[0], outs[1:])


def _rms_call(x, g, B, S, tm):
    T, D = x.shape
    specs, shapes = _u_out_specs(T, D, B, S, tm)
    blocks = _nbytes((tm, D), F32) + (1 + len(PERM_DILATIONS)) * _nbytes((tm, D), BF16)
    return pl.pallas_call(
        _rms_kernel,
        grid=(T // tm,),
        in_specs=[pl.BlockSpec((tm, D), lambda i: (i, 0)),
                  pl.BlockSpec((1, D), lambda i: (0, 0))],
        out_specs=specs,
        out_shape=shapes,
        scratch_shapes=[_u_scratch(D, tm, 1)],
        compiler_params=_compiler_params(("parallel",), blocks, 4 * _nbytes((tm, D), F32)),
        name="rms_in",
    )(x, g)


def _project(u_ref, w_ref, o_ref, kinds, chunk, tables):
    cq, sq, ck, sk = tables
    for c, kind in enumerate(kinds):
        acc = jnp.dot(u_ref[...], w_ref[:, c * chunk:(c + 1) * chunk], preferred_element_type=F32)
        if kind in ("rope_q", "rope_k"):
            cos, sin = (cq[...], sq[...]) if kind == "rope_q" else (ck[...], sk[...])
            for hh in range(chunk // HEAD_DIM):
                lo = c * chunk + hh * HEAD_DIM
                x = acc[:, hh * HEAD_DIM:(hh + 1) * HEAD_DIM]
                o_ref[:, lo:lo + HEAD_DIM] = _rope(x, cos, sin).astype(o_ref.dtype)
        else:
            y = acc * ATTN_SCALE if kind == "plain_q" else acc
            o_ref[:, c * chunk:(c + 1) * chunk] = y.astype(o_ref.dtype)


def _in_proj_kernel(u_ref, wg_ref, wq_ref, b_ref, cq_ref, sq_ref, ck_ref, sk_ref, o_ref, *,
                    gate_tiles, tile_kinds, chunk):
    j = pl.program_id(1)

    @pl.when(j < gate_tiles)
    def _gate():
        for c in range(o_ref.shape[1] // chunk):
            cols = slice(c * chunk, (c + 1) * chunk)
            acc = jnp.dot(u_ref[...], wg_ref[:, cols], preferred_element_type=F32)
            o_ref[:, cols] = _sigmoid(acc + b_ref[:, cols]).astype(o_ref.dtype)

    for t, kinds in enumerate(tile_kinds):
        @pl.when(j == gate_tiles + t)
        def _qkv(kinds=kinds):
            _project(u_ref, wq_ref, o_ref, kinds, chunk, (cq_ref, sq_ref, ck_ref, sk_ref))


MAIN_TILE_KINDS = (("rope_q", "rope_k"), ("plain", "plain_q"), ("plain", "plain"))


def _in_proj_call(u, w_gate, w_qkv, b_gate, tables, layer, S, tm, tn):
    T, D = u.shape
    gate_tiles = w_gate.shape[2] // tn
    qkv_tiles = w_qkv.shape[2] // tn
    assert qkv_tiles == len(MAIN_TILE_KINDS) and tn == 2 * GROUP_WIDTH
    pos_blocks = S // tm
    table_spec = pl.BlockSpec((tm, LANES), lambda i, j: (i % pos_blocks, 0))
    blocks = (_nbytes((tm, D), BF16) + 2 * _nbytes((D, tn), BF16) + _nbytes((tm, tn), BF16)
              + 4 * _nbytes((tm, LANES), F32))
    return pl.pallas_call(
        functools.partial(_in_proj_kernel, gate_tiles=gate_tiles, tile_kinds=MAIN_TILE_KINDS, chunk=GROUP_WIDTH),
        grid=(T // tm, gate_tiles + qkv_tiles),
        in_specs=[pl.BlockSpec((tm, D), lambda i, j: (i, 0)),
                  pl.BlockSpec((None, D, tn), lambda i, j: (layer, 0, jnp.minimum(j, gate_tiles - 1))),
                  pl.BlockSpec((None, D, tn), lambda i, j: (layer, 0, jnp.maximum(j - gate_tiles, 0))),
                  pl.BlockSpec((None, 1, tn), lambda i, j: (layer, 0, jnp.minimum(j, gate_tiles - 1))),
                  table_spec, table_spec, table_spec, table_spec],
        out_specs=pl.BlockSpec((tm, tn), lambda i, j: (i, j)),
        out_shape=jax.ShapeDtypeStruct((T, (gate_tiles + qkv_tiles) * tn), BF16),
        compiler_params=_compiler_params(("parallel", "arbitrary"), blocks, 3 * _nbytes((tm, GROUP_WIDTH), F32)),
        name="in_proj",
    )(u, w_gate, w_qkv, b_gate, *tables)


def _in_proj_dil_kernel(u_ref, w_ref, cq_ref, sq_ref, ck_ref, sk_ref, o_ref):
    _project(u_ref, w_ref, o_ref, ("rope_q", "rope_k", "plain"), GROUP_WIDTH, (cq_ref, sq_ref, ck_ref, sk_ref))


def _in_proj_dil_call(u_perm, w, tables_p, layer, S, tm, name):
    T, D = u_perm.shape
    N = w.shape[2]
    pos_blocks = S // tm
    table_spec = pl.BlockSpec((tm, LANES), lambda i: (i % pos_blocks, 0))
    blocks = _nbytes((tm, D), BF16) + _nbytes((tm, N), BF16) + 4 * _nbytes((tm, LANES), F32)
    return pl.pallas_call(
        _in_proj_dil_kernel,
        grid=(T // tm,),
        in_specs=[pl.BlockSpec((tm, D), lambda i: (i, 0)),
                  _resident_spec((None, D, N), lambda i: (layer, 0, 0)),
                  table_spec, table_spec, table_spec, table_spec],
        out_specs=pl.BlockSpec((tm, N), lambda i: (i, 0)),
        out_shape=jax.ShapeDtypeStruct((T, N), BF16),
        compiler_params=_compiler_params(("parallel",), blocks,
                                         _nbytes((D, N), BF16) + 3 * _nbytes((tm, GROUP_WIDTH), F32)),
        name=name,
    )(u_perm, w, *tables_p)


def _decay_kernel(u_ref, w_ref, b_ref, ccol_ref, crow_ref):
    fl = jnp.dot(u_ref[...], w_ref[...], preferred_element_type=F32) + b_ref[...]
    ls = jnp.minimum(fl, 0.0) - jnp.log1p(jnp.exp(-jnp.abs(fl)))
    ri = lax.broadcasted_iota(jnp.int32, (BLOCK, BLOCK), 0)
    ci = lax.broadcasted_iota(jnp.int32, (BLOCK, BLOCK), 1)
    tri = (ci <= ri).astype(F32)
    carry = jnp.zeros((1, LANES), F32)
    for n in range(fl.shape[0] // BLOCK):
        rows = slice(n * BLOCK, (n + 1) * BLOCK)
        blk = jnp.dot(tri, ls[rows], precision=lax.Precision.HIGHEST, preferred_element_type=F32) + carry
        for h in range(N_FOX_HEADS):
            ccol_ref[0, h, rows, :] = jnp.broadcast_to(blk[:, h:h + 1], (BLOCK, LANES))
        crow_ref[0, :, rows] = blk.T[0:SUBLANES, :]
        carry = blk[BLOCK - 1:BLOCK, :]


def _decay_call(u, w, b, layer, B, S):
    T, D = u.shape
    blocks = _nbytes((S, D), BF16) + N_FOX_HEADS * _nbytes((S, LANES), F32) + _nbytes((SUBLANES, S), F32)
    return pl.pallas_call(
        _decay_kernel,
        grid=(B,),
        in_specs=[pl.BlockSpec((S, D), lambda b_: (b_, 0)),
                  _resident_spec((None, D, LANES), lambda b_: (layer, 0, 0)),
                  _resident_spec((None, 1, LANES), lambda b_: (layer, 0, 0))],
        out_specs=[pl.BlockSpec((1, N_FOX_HEADS, S, LANES), lambda b_: (b_, 0, 0, 0)),
                   pl.BlockSpec((1, SUBLANES, S), lambda b_: (b_, 0, 0))],
        out_shape=[jax.ShapeDtypeStruct((B, N_FOX_HEADS, S, LANES), F32),
                   jax.ShapeDtypeStruct((B, SUBLANES, S), F32)],
        compiler_params=_compiler_params(("parallel",), blocks,
                                         _nbytes((D, LANES), BF16) + 4 * _nbytes((S, LANES), F32)),
        name="fox_decay",
    )(u, w, b)


def _dil_kernel(q_ref, k_ref, v_ref, o_ref, lse_ref, *, dilation, nb, diag_group, band_group):
    qi = lax.broadcasted_iota(jnp.int32, (BLOCK, 2 * BLOCK), 0)
    ki = lax.broadcasted_iota(jnp.int32, (BLOCK, 2 * BLOCK), 1)
    band = (ki >= qi) & (ki <= qi + BLOCK)
    qd = lax.broadcasted_iota(jnp.int32, (BLOCK, BLOCK), 0)
    kd = lax.broadcasted_iota(jnp.int32, (BLOCK, BLOCK), 1)
    diag = kd <= qd

    def run(units, klen, mask):
        jobs = [(r, q0, k0, h) for (r, q0, k0) in units for h in range(HEADS_PER_DIL)]
        scores = []
        for r, q0, k0, h in jobs:
            cs = pl.ds(h * HEAD_DIM, HEAD_DIM)
            s = lax.dot_general(q_ref[0, r, pl.ds(q0, BLOCK), cs], k_ref[0, r, pl.ds(k0, klen), cs],
                                (((1,), (1,)), ((), ())), preferred_element_type=F32)
            scores.append(jnp.where(mask, s, -jnp.inf))
        probs = []
        for s in scores:
            m = jnp.max(s, axis=-1, keepdims=True)
            p = jnp.exp(s - m)
            l = jnp.sum(p, axis=-1, keepdims=True)
            probs.append((p.astype(BF16), 1.0 / l, m + jnp.log(l)))
        for (r, q0, k0, h), (p, inv_l, lse) in zip(jobs, probs):
            v = v_ref[0, r, pl.ds(k0, klen), pl.ds(h * HEAD_DIM, HEAD_DIM)]
            o = jnp.dot(p, v, preferred_element_type=F32) * inv_l
            out_rows = pl.ds(q0 * dilation + r, BLOCK, stride=dilation) if dilation > 1 else pl.ds(q0, BLOCK)
            o_ref[0, h, out_rows, :] = o
            lse_ref[0, h, out_rows, :] = jnp.broadcast_to(lse, (BLOCK, HEAD_DIM))

    def loop(trips, body):
        if trips == 1:
            body(0)
        elif trips > 1:
            lax.fori_loop(0, trips, lambda t, c: (body(t), c)[1], 0)

    loop(dilation // diag_group,
         lambda g: run([(g * diag_group + t, 0, 0) for t in range(diag_group)], BLOCK, diag))

    def band_blocks(r):
        def body(g):
            units = []
            for t in range(band_group):
                q0 = (1 + g * band_group + t) * BLOCK
                k0 = q0 - BLOCK
                if not isinstance(q0, int):
                    q0, k0 = pl.multiple_of(q0, BLOCK), pl.multiple_of(k0, BLOCK)
                units.append((r, q0, k0))
            run(units, 2 * BLOCK, band)

        loop((nb - 1) // band_group, body)

    loop(dilation, band_blocks)


def _dil_call(z, col0, dilation, B, S):
    T, N = z.shape
    L = S // dilation
    nb = L // BLOCK
    zv = z.reshape(B, dilation, L, N)
    cb = col0 // GROUP_WIDTH

    def spec(c):
        return pl.BlockSpec((1, dilation, L, GROUP_WIDTH), lambda b_: (b_, 0, 0, cb + c))

    out_spec = pl.BlockSpec((1, HEADS_PER_DIL, S, HEAD_DIM), lambda b_: (b_, 0, 0, 0))
    out_sds = jax.ShapeDtypeStruct((B, HEADS_PER_DIL, S, HEAD_DIM), F32)
    blocks = 3 * _nbytes((S, GROUP_WIDTH), BF16) + 2 * _nbytes((S, GROUP_WIDTH), F32)
    return pl.pallas_call(
        functools.partial(_dil_kernel, dilation=dilation, nb=nb, diag_group=min(dilation, 4),
                          band_group=3 if (nb - 1) % 3 == 0 else 1),
        grid=(B,),
        in_specs=[spec(0), spec(1), spec(2)],
        out_specs=[out_spec, out_spec],
        out_shape=[out_sds, out_sds],
        compiler_params=_compiler_params(("parallel",), blocks),
        name=f"dilated_d{dilation}",
    )(zv, zv, zv)


def _fox_kernel(q_ref, k_ref, v_ref, cq_ref, ck_ref, o_ref, acc_ref, m_ref, *, tile):
    i = pl.program_id(1)
    qd = lax.broadcasted_iota(jnp.int32, (tile, tile), 0)
    kd = lax.broadcasted_iota(jnp.int32, (tile, tile), 1)
    causal = kd <= qd
    heads = [pl.ds(h * HEAD_DIM, HEAD_DIM) for h in range(N_FOX_HEADS)]
    slabs = [slice(c * LANES, (c + 1) * LANES) for c in range(tile // LANES)]
    ones = jnp.ones((tile, HEAD_DIM), BF16)
    for h in range(N_FOX_HEADS):
        m_ref[h] = jnp.full((tile, LANES), -jnp.inf, F32)
        acc_ref[h] = jnp.zeros((tile, 2 * HEAD_DIM), F32)

    def step(t, masked):
        k0 = pl.multiple_of(t * tile, tile)
        logits = []
        for h, cs in enumerate(heads):
            s = lax.dot_general(q_ref[0, :, cs], k_ref[0, pl.ds(k0, tile), cs], (((1,), (1,)), ((), ())),
                                preferred_element_type=F32)
            s = s - ck_ref[0, h:h + 1, pl.ds(k0, tile)]
            logits.append(jnp.where(causal, s, -jnp.inf) if masked else s)
        probs = []
        for h, s in enumerate(logits):
            cq = cq_ref[0, h]
            m_old = m_ref[h]
            row_max = jnp.broadcast_to(jnp.max(s, axis=-1, keepdims=True), (tile, LANES))
            m_new = jnp.maximum(m_old, row_max + cq)
            m_ref[h] = m_new
            shift = cq - m_new
            p = jnp.concatenate([jnp.exp(s[:, sl] + shift) for sl in slabs], axis=1).astype(BF16)
            probs.append((p, jnp.exp(m_old - m_new)))
        for h, (cs, (p, alpha)) in enumerate(zip(heads, probs)):
            v_ones = jnp.concatenate([v_ref[0, pl.ds(k0, tile), cs], ones], axis=1)
            pv = jnp.dot(p, v_ones, preferred_element_type=F32)
            for sl in (slice(0, HEAD_DIM), slice(HEAD_DIM, 2 * HEAD_DIM)):
                acc_ref[h, :, sl] = alpha * acc_ref[h, :, sl] + pv[:, sl]

    lax.fori_loop(0, i, lambda t, c: (step(t, False), c)[1], 0)
    step(i, True)
    for h, cs in enumerate(heads):
        acc = acc_ref[h]
        o_ref[0, :, cs] = (acc[:, :HEAD_DIM] * (1.0 / acc[:, HEAD_DIM:])).astype(o_ref.dtype)


def _fox_call(z, col0, ccol, crow, B, S, tile):
    T, N = z.shape
    zv = z.reshape(B, S, N)
    base = col0 // GROUP_WIDTH
    blocks = (_nbytes((tile, GROUP_WIDTH), BF16) * 2 + 2 * _nbytes((S, GROUP_WIDTH), BF16)
              + N_FOX_HEADS * _nbytes((tile, LANES), F32) + _nbytes((SUBLANES, S), F32))
    scratch = [pltpu.VMEM((N_FOX_HEADS, tile, 2 * HEAD_DIM), F32), pltpu.VMEM((N_FOX_HEADS, tile, LANES), F32)]
    out = pl.pallas_call(
        functools.partial(_fox_kernel, tile=tile),
        grid=(B, S // tile),
        in_specs=[pl.BlockSpec((1, tile, GROUP_WIDTH), lambda b_, i: (b_, i, base)),
                  pl.BlockSpec((1, S, GROUP_WIDTH), lambda b_, i: (b_, 0, base + 1)),
                  pl.BlockSpec((1, S, GROUP_WIDTH), lambda b_, i: (b_, 0, base + 2)),
                  pl.BlockSpec((1, N_FOX_HEADS, tile, LANES), lambda b_, i: (b_, 0, i, 0)),
                  pl.BlockSpec((1, SUBLANES, S), lambda b_, i: (b_, 0, 0))],
        out_specs=pl.BlockSpec((1, tile, GROUP_WIDTH), lambda b_, i: (b_, i, 0)),
        out_shape=jax.ShapeDtypeStruct((B, S, GROUP_WIDTH), BF16),
        scratch_shapes=scratch,
        compiler_params=_compiler_params(("parallel", "arbitrary"), blocks,
                                         N_FOX_HEADS * (2 * _nbytes((tile, 2 * HEAD_DIM), F32)
                                                        + _nbytes((tile, LANES), F32) + 2 * _nbytes((tile, tile), F32))),
        name="fox_attention",
    )(zv, zv, zv, ccol, crow)
    return out.reshape(T, GROUP_WIDTH)


def _merge_kernel(o0_ref, o1_ref, o2_ref, l0_ref, l1_ref, l2_ref, yb_ref, ga_ref, gb_ref, h_ref,
                  wa_ref, wb_ref, wo_ref, h1_ref):
    heads = []
    for hd in range(HEADS_PER_DIL):
        l0, l1, l2 = l0_ref[0, hd], l1_ref[0, hd], l2_ref[0, hd]
        mx = jnp.maximum(jnp.maximum(l0, l1), l2)
        e0, e1, e2 = jnp.exp(l0 - mx), jnp.exp(l1 - mx), jnp.exp(l2 - mx)
        inv = 1.0 / (e0 + e1 + e2)
        ya_h = (e0 * inv) * o0_ref[0, hd] + (e1 * inv) * o1_ref[0, hd] + (e2 * inv) * o2_ref[0, hd]
        heads.append(ya_h.astype(BF16))
    ya = jnp.concatenate(heads, axis=1)
    ya_p = jnp.dot(ya, wa_ref[...], preferred_element_type=F32)
    yb_p = jnp.dot(yb_ref[...], wb_ref[...], preferred_element_type=F32)
    merged = ga_ref[...].astype(F32) * ya_p + gb_ref[...].astype(F32) * yb_p
    h1_ref[...] = h_ref[...] + jnp.dot(merged.astype(BF16), wo_ref[...], preferred_element_type=F32)


def _merge_call(os_, ls_, yb, z, h, wa, wb, wo, layer, S, tm):
    T, D = h.shape
    row = lambda i: (i, 0)
    per_seq = S // tm
    hw = pl.BlockSpec((1, HEADS_PER_DIL, tm, HEAD_DIM), lambda i: (i // per_seq, 0, i % per_seq, 0))
    blocks = (6 * _nbytes((tm, GROUP_WIDTH), F32) + _nbytes((tm, GROUP_WIDTH), BF16) + 2 * _nbytes((tm, D), BF16)
              + 2 * _nbytes((tm, D), F32))
    weights = 2 * _nbytes((GROUP_WIDTH, D), BF16) + _nbytes((D, D), BF16)
    return pl.pallas_call(
        _merge_kernel,
        grid=(T // tm,),
        in_specs=[hw, hw, hw, hw, hw, hw, pl.BlockSpec((tm, GROUP_WIDTH), row),
                  pl.BlockSpec((tm, D), lambda i: (i, 0)),
                  pl.BlockSpec((tm, D), lambda i: (i, 1)),
                  pl.BlockSpec((tm, D), row),
                  _resident_spec((None, GROUP_WIDTH, D), lambda i: (layer, 0, 0)),
                  _resident_spec((None, GROUP_WIDTH, D), lambda i: (layer, 0, 0)),
                  _resident_spec((None, D, D), lambda i: (layer, 0, 0))],
        out_specs=pl.BlockSpec((tm, D), row),
        out_shape=jax.ShapeDtypeStruct((T, D), F32),
        compiler_params=_compiler_params(("parallel",), blocks, weights + 3 * _nbytes((tm, D), F32)),
        name="merge_wo",
    )(*os_, *ls_, yb, z, z, h, wa, wb, wo)


def _mlp_kernel(h_ref, g_ref, wu_ref, wd_ref, o_ref, m_ref):
    j = pl.program_id(1)

    @pl.when(j == 0)
    def _init():
        h = h_ref[...]
        m_ref[...] = _rms(h, g_ref[...]).astype(m_ref.dtype)
        o_ref[...] = h

    a = jnp.dot(m_ref[...], wu_ref[...], preferred_element_type=F32)
    a = jnp.square(jnp.maximum(a, 0.0)).astype(BF16)
    o_ref[...] += jnp.dot(a, wd_ref[...], preferred_element_type=F32)


def _mlp_call(h, g, wu, wd, layer, tm, tf):
    T, D = h.shape
    FF = wu.shape[2]
    blocks = 2 * _nbytes((tm, D), F32) + 2 * _nbytes((D, tf), BF16)
    resident = _nbytes((tm, D), BF16) + _nbytes((tm, tf), F32) + _nbytes((tm, tf), BF16)
    return pl.pallas_call(
        _mlp_kernel,
        grid=(T // tm, FF // tf),
        in_specs=[pl.BlockSpec((tm, D), lambda i, j: (i, 0)),
                  pl.BlockSpec((None, 1, D), lambda i, j: (layer, 0, 0)),
                  pl.BlockSpec((None, D, tf), lambda i, j: (layer, 0, j)),
                  pl.BlockSpec((None, tf, D), lambda i, j: (layer, j, 0))],
        out_specs=pl.BlockSpec((tm, D), lambda i, j: (i, 0)),
        out_shape=jax.ShapeDtypeStruct((T, D), F32),
        scratch_shapes=[pltpu.VMEM((tm, D), BF16)],
        compiler_params=_compiler_params(("parallel", "arbitrary"), blocks, resident),
        name="mlp",
    )(h, g, wu, wd)


def _ple_kernel(h_ref, p_ref, g_ref, gn_ref, wg_ref, wp_ref, *refs, last, n_sub):
    rows = h_ref.shape[0] // n_sub
    subs = [slice(t * rows, (t + 1) * rows) for t in range(n_sub)]
    normed = [_rms(h_ref[rs, :], g_ref[...]).astype(BF16) for rs in subs]
    parts = [(jnp.dot(n, wg_ref[...], preferred_element_type=F32),
              jnp.dot(p_ref[rs, :].astype(BF16), wp_ref[...], preferred_element_type=F32))
             for n, rs in zip(normed, subs)]
    for t, (rs, (gate_logit, pp)) in enumerate(zip(subs, parts)):
        h3 = h_ref[rs, :] + _sigmoid(gate_logit) * pp
        if last:
            refs[0][rs, :] = _rms(h3, gn_ref[...])
        else:
            refs[0][rs, :] = h3
            _emit_u(_rms(h3, gn_ref[...]), t, refs[-1], refs[1], refs[2:-1])


def _ple_call(h, p, g, g_next, wg, wp, layer, B, S, tm, n_sub, last):
    T, D = h.shape
    P = p.shape[2]
    row = lambda i: (i, 0)
    if last:
        out_specs = [pl.BlockSpec((tm, D), row)]
        out_shape = [jax.ShapeDtypeStruct((T, D), F32)]
        out_bytes = _nbytes((tm, D), F32)
        scratch = []
    else:
        u_specs, u_shapes = _u_out_specs(T, D, B, S, tm)
        out_specs = [pl.BlockSpec((tm, D), row)] + u_specs
        out_shape = [jax.ShapeDtypeStruct((T, D), F32)] + u_shapes
        out_bytes = _nbytes((tm, D), F32) + (1 + len(PERM_DILATIONS)) * _nbytes((tm, D), BF16)
        scratch = [_u_scratch(D, tm, n_sub)]
    blocks = _nbytes((tm, D), F32) + _nbytes((tm, P), F32) + out_bytes
    weights = _nbytes((D, D), BF16) + _nbytes((P, D), BF16)
    return pl.pallas_call(
        functools.partial(_ple_kernel, last=last, n_sub=n_sub),
        grid=(T // tm,),
        in_specs=[pl.BlockSpec((tm, D), row),
                  pl.BlockSpec((None, tm, P), lambda i: (layer, i, 0)),
                  pl.BlockSpec((None, 1, D), lambda i: (layer, 0, 0)),
                  pl.BlockSpec((1, D), lambda i: (0, 0)),
                  _resident_spec((None, D, D), lambda i: (layer, 0, 0)),
                  _resident_spec((None, P, D), lambda i: (layer, 0, 0))],
        out_specs=out_specs,
        out_shape=out_shape,
        scratch_shapes=scratch,
        compiler_params=_compiler_params(("parallel",), blocks, weights + 4 * _nbytes((tm, D), F32)),
        name="ple_norm",
    )(h, p, g, g_next, wg, wp)


def _rope_tables(S):
    inv = ROPE_THETA ** (-jnp.arange(ROPE_HALF, dtype=F32) / ROPE_HALF)
    ang = jnp.arange(S, dtype=F32)[:, None] * inv[None, :]
    cos, sin = jnp.cos(ang), jnp.sin(ang)
    ones = jnp.ones((S, HEAD_DIM // 2 - ROPE_HALF), F32)
    zeros = jnp.zeros((S, HEAD_DIM // 2 - ROPE_HALF), F32)
    cos_t = jnp.concatenate([cos, ones, cos, ones], axis=1)
    sin_t = jnp.concatenate([-sin, zeros, sin, zeros], axis=1)
    return cos_t * ATTN_SCALE, sin_t * ATTN_SCALE, cos_t, sin_t


def _residue_major(t, d):
    S = t.shape[0]
    return t.reshape(S // d, d, t.shape[1]).transpose(1, 0, 2).reshape(S, t.shape[1])


def kernel(x, p, g_mix, w_in, b_f, w_gate, b_gate, w_br_a, w_br_b, w_o, g_mlp, w_up, w_down, g_ple, w_ple,
           w_ple_gate, g_final):
    B, S, D = x.shape
    depth = p.shape[0]
    T = B * S
    A, G = ATTN_WIDTH, GROUP_WIDTH
    assert all(w // d == BLOCK and S % (d * BLOCK) == 0 for w, d in DIL_GROUPS) and DIL_GROUPS[0][1] == 1
    assert all(b % a == 0 for a, b in zip((1,) + PERM_DILATIONS, PERM_DILATIONS))
    assert w_in.shape[2] == 3 * A + N_FOX_HEADS and w_gate.shape[2] == 2 * D

    tm_in, tn_in = min(1024, S), 2 * G
    tm_row = 256
    tm_mlp, tf_mlp = 1024, 512
    tm_ple, sub_ple = 512, 2
    fox_tile = 256

    fox = N_GROUPS - 1
    w_qkv = _qkv_prep_call(w_in, lambda s: N_GROUPS * (s % 3) + fox * (s // 3), 6, 2, "qkv_prep_main")
    w_dil = [_qkv_prep_call(w_in, lambda s, g=g: N_GROUPS * s + g, 3, 2, f"qkv_prep_d{d}")
             for g, (_, d) in enumerate(DIL_GROUPS) if d > 1]
    w_gate_b, b_gate3 = w_gate.astype(BF16), b_gate[:, None, :]
    w_f = jnp.pad(w_in[:, :, 3 * A:], ((0, 0), (0, 0), (0, LANES - N_FOX_HEADS))).astype(BF16)
    b_fp = jnp.pad(b_f, ((0, 0), (0, LANES - N_FOX_HEADS)))[:, None, :]
    wa, wb, wo = w_br_a.astype(BF16), w_br_b.astype(BF16), w_o.astype(BF16)
    wu, wd = w_up.astype(BF16), w_down.astype(BF16)
    wpg, wp = w_ple_gate.astype(BF16), w_ple.astype(BF16)
    g_mlp3, g_ple3 = g_mlp[:, None, :], g_ple[:, None, :]
    p3 = p.reshape(depth, T, p.shape[3])
    tables = _rope_tables(S)
    tables_p = [tuple(_residue_major(t, d) for t in tables) for d in PERM_DILATIONS]
    qkv0_col = w_gate.shape[2]
    fox_cols = (qkv0_col + 3 * G, qkv0_col + 4 * G, qkv0_col + 5 * G)
    assert fox_cols[1] == fox_cols[0] + G and fox_cols[2] == fox_cols[1] + G

    h = x.reshape(T, D)
    u, *u_perm = _rms_call(h, g_mix[0][None, :], B, S, tm_row)
    for layer in range(depth):
        z = _in_proj_call(u, w_gate_b, w_qkv, b_gate3, tables, layer, S, tm_in, tn_in)
        ccol, crow = _decay_call(u, w_f, b_fp, layer, B, S)
        dil = [_dil_call(z, qkv0_col, 1, B, S)]
        for up, w_g, tp, d in zip(u_perm, w_dil, tables_p, PERM_DILATIONS):
            z_g = _in_proj_dil_call(up.reshape(T, D), w_g, tp, layer, S, tm_in, f"in_proj_d{d}")
            dil.append(_dil_call(z_g, 0, d, B, S))
        yb = _fox_call(z, fox_cols[0], ccol, crow, B, S, fox_tile)
        h = _merge_call([o for o, _ in dil], [l for _, l in dil], yb, z, h, wa, wb, wo, layer, S, tm_row)
        h = _mlp_call(h, g_mlp3, wu, wd, layer, tm_mlp, tf_mlp)
        last = layer == depth - 1
        g_next = g_final[None, :] if last else g_mix[layer + 1][None, :]
        outs = _ple_call(h, p3, g_ple3, g_next, wpg, wp, layer, B, S, tm_ple, sub_ple, last)
        if last:
            return outs[0].reshape(B, S, D)
        h, u, *u_perm = outs
```

```python
import functools

import jax
import jax.numpy as jnp
from jax import lax
from jax.experimental import pallas as pl
from jax.experimental.pallas import tpu as pltpu

HEAD_DIM = 128
BLOCK = 128
DIL_GROUPS = ((128, 1), (512, 4), (2048, 16))
HEADS_PER_DIL = 4
N_DIL_HEADS = HEADS_PER_DIL * len(DIL_GROUPS)
N_FOX_HEADS = 4
N_HEADS = N_DIL_HEADS + N_FOX_HEADS
ATTN_WIDTH = N_HEADS * HEAD_DIM
GROUP_WIDTH = HEADS_PER_DIL * HEAD_DIM
N_GROUPS = ATTN_WIDTH // GROUP_WIDTH
ROPE_THETA = 500000.0
ROPE_DIM = HEAD_DIM // 4
ROPE_HALF = ROPE_DIM // 2
NORM_EPS = 1e-6
ATTN_SCALE = HEAD_DIM ** -0.5

LANES = 128
SUBLANES = 8
BF16_ROW_TILE = 2 * SUBLANES
V7X_VMEM_LIMIT_CAP = 58 * 1024 * 1024

F32 = jnp.float32
BF16 = jnp.bfloat16

PERM_DILATIONS = tuple(d for _, d in DIL_GROUPS if d > 1)


def _nbytes(shape, dtype):
    n = 1
    for s in shape:
        n *= s
    return n * jnp.dtype(dtype).itemsize


def _compiler_params(semantics, pipelined, resident=0):
    need = 2 * pipelined + resident
    assert need <= V7X_VMEM_LIMIT_CAP, (need, V7X_VMEM_LIMIT_CAP)
    return pltpu.CompilerParams(dimension_semantics=semantics, vmem_limit_bytes=V7X_VMEM_LIMIT_CAP)


def _resident_spec(block_shape, index_map):
    return pl.BlockSpec(block_shape, index_map, pipeline_mode=pl.Buffered(1))


def _rms(x, g):
    return x * lax.rsqrt(jnp.mean(x * x, axis=-1, keepdims=True) + NORM_EPS) * g


def _sigmoid(x):
    return 0.5 * jnp.tanh(0.5 * x) + 0.5


def _rope(x, c, s):
    return x * c + pltpu.roll(x, HEAD_DIM // 2, 1) * s


def _emit_u(u, sub, scr_ref, u_ref, perm_refs):
    rows, n_chunks = u.shape[0], u.shape[1] // LANES
    row0 = sub * rows
    u_ref[row0:row0 + rows, :] = u.astype(u_ref.dtype)
    n_sub = scr_ref.shape[0] // (len(perm_refs) * n_chunks)

    def slot(level, c):
        return (level * n_sub + sub) * n_chunks + c

    for c in range(n_chunks):
        scr_ref[slot(0, c)] = u[:, c * LANES:(c + 1) * LANES]
    d_prev = 1
    for level, ref in enumerate(perm_refs):
        d = ref.shape[1]
        f, steps = d // d_prev, rows // d
        for c in range(n_chunks):
            lanes = slice(c * LANES, (c + 1) * LANES)
            for b in range(f):
                part = scr_ref[slot(level, c), pl.ds(b, rows // f, stride=f), :]
                if level + 1 < len(perm_refs):
                    scr_ref[slot(level + 1, c), b * (rows // f):(b + 1) * (rows // f), :] = part
                for rr in range(d // f):
                    ref[0, b * (d // f) + rr, row0 // d:row0 // d + steps, lanes] = (
                        part[rr * steps:(rr + 1) * steps, :].astype(ref.dtype))
        d_prev = d


def _u_scratch(D, tm, n_sub):
    return pltpu.VMEM((len(PERM_DILATIONS) * n_sub * (D // LANES), tm // n_sub, LANES), F32)


def _u_out_specs(T, D, B, S, tm):
    per_seq = S // tm
    specs = [pl.BlockSpec((tm, D), lambda i: (i, 0))]
    shapes = [jax.ShapeDtypeStruct((T, D), BF16)]
    for d in PERM_DILATIONS:
        specs.append(pl.BlockSpec((1, d, tm // d, D), lambda i: (i // per_seq, 0, i % per_seq, 0)))
        shapes.append(jax.ShapeDtypeStruct((B, d, S // d, D), BF16))
    return specs, shapes


def _qkv_prep_kernel(w_ref, o_ref, *, rope_blocks):
    s = pl.program_id(1)

    @pl.when(s < rope_blocks)
    def _rotary_heads():
        lane = lax.broadcasted_iota(jnp.int32, (w_ref.shape[1], HEAD_DIM), 1)
        mid = HEAD_DIM // 2
        for hh in range(o_ref.shape[2] // HEAD_DIM):
            cols = slice(hh * HEAD_DIM, (hh + 1) * HEAD_DIM)
            x = w_ref[0, :, cols].astype(F32)
            up = pltpu.roll(x, HEAD_DIM - ROPE_HALF, 1)
            down = pltpu.roll(x, mid - ROPE_HALF, 1)
            y = jnp.where(lane < ROPE_HALF, x,
                          jnp.where(lane < mid, up, jnp.where(lane < mid + ROPE_HALF, down, x)))
            o_ref[0, :, cols] = y.astype(o_ref.dtype)

    @pl.when(s >= rope_blocks)
    def _plain():
        o_ref[...] = w_ref[...]


def _qkv_prep_call(w_in, src_block, n_blocks, rope_blocks, name):
    depth, D, _ = w_in.shape
    G = GROUP_WIDTH
    return pl.pallas_call(
        functools.partial(_qkv_prep_kernel, rope_blocks=rope_blocks),
        grid=(depth, n_blocks),
        in_specs=[pl.BlockSpec((1, D, G), lambda l, s: (l, 0, src_block(s)))],
        out_specs=pl.BlockSpec((1, D, G), lambda l, s: (l, 0, s)),
        out_shape=jax.ShapeDtypeStruct((depth, D, n_blocks * G), BF16),
        compiler_params=_compiler_params(("parallel", "arbitrary"), 2 * _nbytes((D, G), BF16),
                                         3 * _nbytes((D, HEAD_DIM), F32)),
        name=name,
    )(w_in)


def _rms_kernel(x_ref, g_ref, *refs):
    outs, scr_ref = refs[:-1], refs[-1]
    _emit_u(_rms(x_ref[...], g_ref[...]), 0, scr_ref, outs[0], outs[1:])


def _rms_call(x, g, B, S, tm):
    T, D = x.shape
    specs, shapes = _u_out_specs(T, D, B, S, tm)
    blocks = _nbytes((tm, D), F32) + (1 + len(PERM_DILATIONS)) * _nbytes((tm, D), BF16)
    return pl.pallas_call(
        _rms_kernel,
        grid=(T // tm,),
        in_specs=[pl.BlockSpec((tm, D), lambda i: (i, 0)),
                  pl.BlockSpec((1, D), lambda i: (0, 0))],
        out_specs=specs,
        out_shape=shapes,
        scratch_shapes=[_u_scratch(D, tm, 1)],
        compiler_params=_compiler_params(("parallel",), blocks, 4 * _nbytes((tm, D), F32)),
        name="rms_in",
    )(x, g)


def _project(u_ref, w_ref, o_ref, kinds, chunk, tables):
    cq, sq, ck, sk = tables
    for c, kind in enumerate(kinds):
        acc = jnp.dot(u_ref[...], w_ref[:, c * chunk:(c + 1) * chunk], preferred_element_type=F32)
        if kind in ("rope_q", "rope_k"):
            cos, sin = (cq[...], sq[...]) if kind == "rope_q" else (ck[...], sk[...])
            for hh in range(chunk // HEAD_DIM):
                lo = c * chunk + hh * HEAD_DIM
                x = acc[:, hh * HEAD_DIM:(hh + 1) * HEAD_DIM]
                o_ref[:, lo:lo + HEAD_DIM] = _rope(x, cos, sin).astype(o_ref.dtype)
        else:
            y = acc * ATTN_SCALE if kind == "plain_q" else acc
            o_ref[:, c * chunk:(c + 1) * chunk] = y.astype(o_ref.dtype)


def _gate_kernel(u_ref, w_ref, b_ref, o_ref, *, chunk):
    for c in range(o_ref.shape[1] // chunk):
        cols = slice(c * chunk, (c + 1) * chunk)
        acc = jnp.dot(u_ref[...], w_ref[:, cols], preferred_element_type=F32)
        o_ref[:, cols] = _sigmoid(acc + b_ref[:, cols]).astype(o_ref.dtype)


def _gate_call(u, w, b, layer, tm):
    T, D = u.shape
    N = w.shape[2]
    blocks = _nbytes((tm, D), BF16) + _nbytes((tm, N), BF16) + _nbytes((1, N), F32)
    return pl.pallas_call(
        functools.partial(_gate_kernel, chunk=GROUP_WIDTH),
        grid=(T // tm,),
        in_specs=[pl.BlockSpec((tm, D), lambda i: (i, 0)),
                  _resident_spec((None, D, N), lambda i: (layer, 0, 0)),
                  pl.BlockSpec((None, 1, N), lambda i: (layer, 0, 0))],
        out_specs=pl.BlockSpec((tm, N), lambda i: (i, 0)),
        out_shape=jax.ShapeDtypeStruct((T, N), BF16),
        compiler_params=_compiler_params(("parallel",), blocks,
                                         _nbytes((D, N), BF16) + 3 * _nbytes((tm, GROUP_WIDTH), F32)),
        name="in_gate",
    )(u, w, b)


MAIN_KINDS = ("rope_q", "rope_k", "plain", "plain_q", "plain", "plain")
DIL_KINDS = ("rope_q", "rope_k", "plain")


def _qkv_kernel(u_ref, w_ref, cq_ref, sq_ref, ck_ref, sk_ref, o_ref, *, kinds):
    _project(u_ref, w_ref, o_ref, kinds, GROUP_WIDTH, (cq_ref, sq_ref, ck_ref, sk_ref))


def _qkv_call(u_rows, w, tables_rows, kinds, layer, S, tm, name):
    T, D = u_rows.shape
    N = w.shape[2]
    assert N == len(kinds) * GROUP_WIDTH
    pos_blocks = S // tm
    table_spec = pl.BlockSpec((tm, LANES), lambda i: (i % pos_blocks, 0))
    blocks = _nbytes((tm, D), BF16) + _nbytes((tm, N), BF16) + 4 * _nbytes((tm, LANES), F32)
    return pl.pallas_call(
        functools.partial(_qkv_kernel, kinds=kinds),
        grid=(T // tm,),
        in_specs=[pl.BlockSpec((tm, D), lambda i: (i, 0)),
                  _resident_spec((None, D, N), lambda i: (layer, 0, 0)),
                  table_spec, table_spec, table_spec, table_spec],
        out_specs=pl.BlockSpec((tm, N), lambda i: (i, 0)),
        out_shape=jax.ShapeDtypeStruct((T, N), BF16),
        compiler_params=_compiler_params(("parallel",), blocks,
                                         _nbytes((D, N), BF16) + 3 * _nbytes((tm, GROUP_WIDTH), F32)),
        name=name,
    )(u_rows, w, *tables_rows)


def _decay_kernel(u_ref, w_ref, b_ref, ccol_ref, crow_ref):
    fl = jnp.dot(u_ref[...], w_ref[...], preferred_element_type=F32) + b_ref[...]
    ls = jnp.minimum(fl, 0.0) - jnp.log1p(jnp.exp(-jnp.abs(fl)))
    ri = lax.broadcasted_iota(jnp.int32, (BLOCK, BLOCK), 0)
    ci = lax.broadcasted_iota(jnp.int32, (BLOCK, BLOCK), 1)
    tri = (ci <= ri).astype(F32)
    carry = jnp.zeros((1, LANES), F32)
    for n in range(fl.shape[0] // BLOCK):
        rows = slice(n * BLOCK, (n + 1) * BLOCK)
        blk = jnp.dot(tri, ls[rows], precision=lax.Precision.HIGHEST, preferred_element_type=F32) + carry
        for h in range(N_FOX_HEADS):
            ccol_ref[0, h, rows, :] = jnp.broadcast_to(blk[:, h:h + 1], (BLOCK, LANES))
        crow_ref[0, :, rows] = blk.T[0:SUBLANES, :]
        carry = blk[BLOCK - 1:BLOCK, :]


def _decay_call(u, w, b, layer, B, S):
    T, D = u.shape
    blocks = _nbytes((S, D), BF16) + N_FOX_HEADS * _nbytes((S, LANES), F32) + _nbytes((SUBLANES, S), F32)
    return pl.pallas_call(
        _decay_kernel,
        grid=(B,),
        in_specs=[pl.BlockSpec((S, D), lambda b_: (b_, 0)),
                  _resident_spec((None, D, LANES), lambda b_: (layer, 0, 0)),
                  _resident_spec((None, 1, LANES), lambda b_: (layer, 0, 0))],
        out_specs=[pl.BlockSpec((1, N_FOX_HEADS, S, LANES), lambda b_: (b_, 0, 0, 0)),
                   pl.BlockSpec((1, SUBLANES, S), lambda b_: (b_, 0, 0))],
        out_shape=[jax.ShapeDtypeStruct((B, N_FOX_HEADS, S, LANES), F32),
                   jax.ShapeDtypeStruct((B, SUBLANES, S), F32)],
        compiler_params=_compiler_params(("parallel",), blocks,
                                         _nbytes((D, LANES), BF16) + 4 * _nbytes((S, LANES), F32)),
        name="fox_decay",
    )(u, w, b)


def _dil_kernel(q_ref, k_ref, v_ref, o_ref, lse_ref, *, dilation, nb, diag_group, band_group):
    qi = lax.broadcasted_iota(jnp.int32, (BLOCK, 2 * BLOCK), 0)
    ki = lax.broadcasted_iota(jnp.int32, (BLOCK, 2 * BLOCK), 1)
    band = (ki >= qi) & (ki <= qi + BLOCK)
    qd = lax.broadcasted_iota(jnp.int32, (BLOCK, BLOCK), 0)
    kd = lax.broadcasted_iota(jnp.int32, (BLOCK, BLOCK), 1)
    diag = kd <= qd

    def run(units, klen, mask):
        jobs = [(r, q0, k0, h) for (r, q0, k0) in units for h in range(HEADS_PER_DIL)]
        scores = []
        for r, q0, k0, h in jobs:
            cs = pl.ds(h * HEAD_DIM, HEAD_DIM)
            s = lax.dot_general(q_ref[0, r, pl.ds(q0, BLOCK), cs], k_ref[0, r, pl.ds(k0, klen), cs],
                                (((1,), (1,)), ((), ())), preferred_element_type=F32)
            scores.append(jnp.where(mask, s, -jnp.inf))
        probs = []
        for s in scores:
            m = jnp.max(s, axis=-1, keepdims=True)
            p = jnp.exp(s - m)
            l = jnp.sum(p, axis=-1, keepdims=True)
            probs.append((p.astype(BF16), 1.0 / l, m + jnp.log(l)))
        for (r, q0, k0, h), (p, inv_l, lse) in zip(jobs, probs):
            v = v_ref[0, r, pl.ds(k0, klen), pl.ds(h * HEAD_DIM, HEAD_DIM)]
            o = jnp.dot(p, v, preferred_element_type=F32) * inv_l
            out_rows = pl.ds(q0 * dilation + r, BLOCK, stride=dilation) if dilation > 1 else pl.ds(q0, BLOCK)
            o_ref[0, h, out_rows, :] = o
            lse_ref[0, h, out_rows, :] = jnp.broadcast_to(lse, (BLOCK, HEAD_DIM))

    def loop(trips, body):
        if trips == 1:
            body(0)
        elif trips > 1:
            lax.fori_loop(0, trips, lambda t, c: (body(t), c)[1], 0)

    loop(dilation // diag_group,
         lambda g: run([(g * diag_group + t, 0, 0) for t in range(diag_group)], BLOCK, diag))

    def band_blocks(r):
        def body(g):
            units = []
            for t in range(band_group):
                q0 = (1 + g * band_group + t) * BLOCK
                k0 = q0 - BLOCK
                if not isinstance(q0, int):
                    q0, k0 = pl.multiple_of(q0, BLOCK), pl.multiple_of(k0, BLOCK)
                units.append((r, q0, k0))
            run(units, 2 * BLOCK, band)

        loop((nb - 1) // band_group, body)

    loop(dilation, band_blocks)


def _dil_call(z, col0, dilation, B, S):
    T, N = z.shape
    L = S // dilation
    nb = L // BLOCK
    zv = z.reshape(B, dilation, L, N)
    cb = col0 // GROUP_WIDTH

    def spec(c):
        return pl.BlockSpec((1, dilation, L, GROUP_WIDTH), lambda b_: (b_, 0, 0, cb + c))

    out_spec = pl.BlockSpec((1, HEADS_PER_DIL, S, HEAD_DIM), lambda b_: (b_, 0, 0, 0))
    out_sds = jax.ShapeDtypeStruct((B, HEADS_PER_DIL, S, HEAD_DIM), F32)
    blocks = 3 * _nbytes((S, GROUP_WIDTH), BF16) + 2 * _nbytes((S, GROUP_WIDTH), F32)
    return pl.pallas_call(
        functools.partial(_dil_kernel, dilation=dilation, nb=nb, diag_group=min(dilation, 4),
                          band_group=3 if (nb - 1) % 3 == 0 else 1),
        grid=(B,),
        in_specs=[spec(0), spec(1), spec(2)],
        out_specs=[out_spec, out_spec],
        out_shape=[out_sds, out_sds],
        compiler_params=_compiler_params(("parallel",), blocks),
        name=f"dilated_d{dilation}",
    )(zv, zv, zv)


def _fox_kernel(q_ref, k_ref, v_ref, cq_ref, ck_ref, o_ref, acc_ref, m_ref, *, tile):
    i = pl.program_id(1)
    qd = lax.broadcasted_iota(jnp.int32, (tile, tile), 0)
    kd = lax.broadcasted_iota(jnp.int32, (tile, tile), 1)
    causal = kd <= qd
    heads = [pl.ds(h * HEAD_DIM, HEAD_DIM) for h in range(N_FOX_HEADS)]
    for h in range(N_FOX_HEADS):
        m_ref[h] = jnp.full((tile, LANES), -jnp.inf, F32)
        acc_ref[h] = jnp.zeros((tile, 2 * HEAD_DIM), F32)

    def step(k0, width, masked):
        k0 = pl.multiple_of(k0, tile)
        slabs = [slice(c * LANES, (c + 1) * LANES) for c in range(width // LANES)]
        ones = jnp.ones((width, HEAD_DIM), BF16)
        logits = []
        for h, cs in enumerate(heads):
            s = lax.dot_general(q_ref[0, :, cs], k_ref[0, pl.ds(k0, width), cs], (((1,), (1,)), ((), ())),
                                preferred_element_type=F32)
            s = s - ck_ref[0, h:h + 1, pl.ds(k0, width)]
            logits.append(jnp.where(causal, s, -jnp.inf) if masked else s)
        probs = []
        for h, s in enumerate(logits):
            cq = cq_ref[0, h]
            m_old = m_ref[h]
            row_max = jnp.broadcast_to(jnp.max(s, axis=-1, keepdims=True), (tile, LANES))
            m_new = jnp.maximum(m_old, row_max + cq)
            m_ref[h] = m_new
            shift = cq - m_new
            p = jnp.concatenate([jnp.exp(s[:, sl] + shift) for sl in slabs], axis=1).astype(BF16)
            probs.append((p, jnp.exp(m_old - m_new)))
        for h, (cs, (p, alpha)) in enumerate(zip(heads, probs)):
            v_ones = jnp.concatenate([v_ref[0, pl.ds(k0, width), cs], ones], axis=1)
            pv = jnp.dot(p, v_ones, preferred_element_type=F32)
            for sl in (slice(0, HEAD_DIM), slice(HEAD_DIM, 2 * HEAD_DIM)):
                acc_ref[h, :, sl] = alpha * acc_ref[h, :, sl] + pv[:, sl]

    lax.fori_loop(0, i // 2, lambda t, c: (step(t * 2 * tile, 2 * tile, False), c)[1], 0)

    @pl.when(i % 2 == 1)
    def _odd():
        step((i - 1) * tile, tile, False)

    step(i * tile, tile, True)
    for h, cs in enumerate(heads):
        acc = acc_ref[h]
        o_ref[0, :, cs] = (acc[:, :HEAD_DIM] * (1.0 / acc[:, HEAD_DIM:])).astype(o_ref.dtype)


def _fox_call(z, col0, ccol, crow, B, S, tile):
    T, N = z.shape
    zv = z.reshape(B, S, N)
    base = col0 // GROUP_WIDTH
    blocks = (_nbytes((tile, GROUP_WIDTH), BF16) * 2 + 2 * _nbytes((S, GROUP_WIDTH), BF16)
              + N_FOX_HEADS * _nbytes((tile, LANES), F32) + _nbytes((SUBLANES, S), F32))
    scratch = [pltpu.VMEM((N_FOX_HEADS, tile, 2 * HEAD_DIM), F32), pltpu.VMEM((N_FOX_HEADS, tile, LANES), F32)]
    out = pl.pallas_call(
        functools.partial(_fox_kernel, tile=tile),
        grid=(B, S // tile),
        in_specs=[pl.BlockSpec((1, tile, GROUP_WIDTH), lambda b_, i: (b_, i, base)),
                  pl.BlockSpec((1, S, GROUP_WIDTH), lambda b_, i: (b_, 0, base + 1)),
                  pl.BlockSpec((1, S, GROUP_WIDTH), lambda b_, i: (b_, 0, base + 2)),
                  pl.BlockSpec((1, N_FOX_HEADS, tile, LANES), lambda b_, i: (b_, 0, i, 0)),
                  pl.BlockSpec((1, SUBLANES, S), lambda b_, i: (b_, 0, 0))],
        out_specs=pl.BlockSpec((1, tile, GROUP_WIDTH), lambda b_, i: (b_, i, 0)),
        out_shape=jax.ShapeDtypeStruct((B, S, GROUP_WIDTH), BF16),
        scratch_shapes=scratch,
        compiler_params=_compiler_params(("parallel", "arbitrary"), blocks,
                                         N_FOX_HEADS * (2 * _nbytes((tile, 2 * HEAD_DIM), F32)
                                                        + _nbytes((tile, LANES), F32) + 2 * _nbytes((tile, tile), F32))),
        name="fox_attention",
    )(zv, zv, zv, ccol, crow)
    return out.reshape(T, GROUP_WIDTH)


def _merge_kernel(o0_ref, o1_ref, o2_ref, l0_ref, l1_ref, l2_ref, yb_ref, ga_ref, gb_ref, h_ref,
                  wa_ref, wb_ref, wo_ref, h1_ref):
    heads = []
    for hd in range(HEADS_PER_DIL):
        l0, l1, l2 = l0_ref[0, hd], l1_ref[0, hd], l2_ref[0, hd]
        mx = jnp.maximum(jnp.maximum(l0, l1), l2)
        e0, e1, e2 = jnp.exp(l0 - mx), jnp.exp(l1 - mx), jnp.exp(l2 - mx)
        inv = 1.0 / (e0 + e1 + e2)
        ya_h = (e0 * inv) * o0_ref[0, hd] + (e1 * inv) * o1_ref[0, hd] + (e2 * inv) * o2_ref[0, hd]
        heads.append(ya_h.astype(BF16))
    ya = jnp.concatenate(heads, axis=1)
    ya_p = jnp.dot(ya, wa_ref[...], preferred_element_type=F32)
    yb_p = jnp.dot(yb_ref[...], wb_ref[...], preferred_element_type=F32)
    merged = ga_ref[...].astype(F32) * ya_p + gb_ref[...].astype(F32) * yb_p
    h1_ref[...] = h_ref[...] + jnp.dot(merged.astype(BF16), wo_ref[...], preferred_element_type=F32)


def _merge_call(os_, ls_, yb, z, h, wa, wb, wo, layer, S, tm):
    T, D = h.shape
    row = lambda i: (i, 0)
    per_seq = S // tm
    hw = pl.BlockSpec((1, HEADS_PER_DIL, tm, HEAD_DIM), lambda i: (i // per_seq, 0, i % per_seq, 0))
    blocks = (6 * _nbytes((tm, GROUP_WIDTH), F32) + _nbytes((tm, GROUP_WIDTH), BF16) + 2 * _nbytes((tm, D), BF16)
              + 2 * _nbytes((tm, D), F32))
    weights = 2 * _nbytes((GROUP_WIDTH, D), BF16) + _nbytes((D, D), BF16)
    return pl.pallas_call(
        _merge_kernel,
        grid=(T // tm,),
        in_specs=[hw, hw, hw, hw, hw, hw, pl.BlockSpec((tm, GROUP_WIDTH), row),
                  pl.BlockSpec((tm, D), lambda i: (i, 0)),
                  pl.BlockSpec((tm, D), lambda i: (i, 1)),
                  pl.BlockSpec((tm, D), row),
                  _resident_spec((None, GROUP_WIDTH, D), lambda i: (layer, 0, 0)),
                  _resident_spec((None, GROUP_WIDTH, D), lambda i: (layer, 0, 0)),
                  _resident_spec((None, D, D), lambda i: (layer, 0, 0))],
        out_specs=pl.BlockSpec((tm, D), row),
        out_shape=jax.ShapeDtypeStruct((T, D), F32),
        compiler_params=_compiler_params(("parallel",), blocks, weights + 3 * _nbytes((tm, D), F32)),
        name="merge_wo",
    )(*os_, *ls_, yb, z, z, h, wa, wb, wo)


def _mlp_kernel(h_ref, g_ref, wu_ref, wd_ref, *refs, cast_next):
    if cast_next:
        next_wu_ref, next_wd_ref, o_ref, cast_wu_ref, cast_wd_ref, m_ref = refs
        cast_wu_ref[...] = next_wu_ref[...].astype(cast_wu_ref.dtype)
        cast_wd_ref[...] = next_wd_ref[...].astype(cast_wd_ref.dtype)
    else:
        o_ref, m_ref = refs
    j = pl.program_id(1)

    @pl.when(j == 0)
    def _init():
        h = h_ref[...]
        m_ref[...] = _rms(h, g_ref[...]).astype(m_ref.dtype)
        o_ref[...] = h

    a = jnp.dot(m_ref[...], wu_ref[...], preferred_element_type=F32)
    a = jnp.square(jnp.maximum(a, 0.0)).astype(BF16)
    o_ref[...] += jnp.dot(a, wd_ref[...], preferred_element_type=F32)


def _cast_chunk(rows, steps):
    assert rows % steps == 0
    per_step = rows // steps
    share = max(1, BF16_ROW_TILE // per_step)
    return per_step * share, share


def _mlp_call(h, g, wu, wd, w_up, w_down, layer, tm, tf):
    T, D = h.shape
    FF = wu.shape[1]
    nj = FF // tf
    cast_next = layer + 1 < w_up.shape[0]
    blocks = 2 * _nbytes((tm, D), F32) + 2 * _nbytes((D, tf), BF16)
    resident = _nbytes((tm, D), BF16) + _nbytes((tm, tf), F32) + _nbytes((tm, tf), BF16)
    in_specs = [pl.BlockSpec((tm, D), lambda i, j: (i, 0)),
                pl.BlockSpec((None, 1, D), lambda i, j: (layer, 0, 0)),
                pl.BlockSpec((D, tf), lambda i, j: (0, j)),
                pl.BlockSpec((tf, D), lambda i, j: (j, 0))]
    out_specs = [pl.BlockSpec((tm, D), lambda i, j: (i, 0))]
    out_shape = [jax.ShapeDtypeStruct((T, D), F32)]
    args = [h, g, wu, wd]
    if cast_next:
        steps = (T // tm) * nj
        (ru, su), (rd, sd) = _cast_chunk(D, steps), _cast_chunk(FF, steps)
        in_specs += [pl.BlockSpec((None, ru, FF), lambda i, j: (layer + 1, (i * nj + j) // su, 0)),
                     pl.BlockSpec((None, rd, D), lambda i, j: (layer + 1, (i * nj + j) // sd, 0))]
        out_specs += [pl.BlockSpec((ru, FF), lambda i, j: ((i * nj + j) // su, 0)),
                      pl.BlockSpec((rd, D), lambda i, j: ((i * nj + j) // sd, 0))]
        out_shape += [jax.ShapeDtypeStruct((D, FF), BF16), jax.ShapeDtypeStruct((FF, D), BF16)]
        args += [w_up, w_down]
        blocks += _nbytes((ru, FF), F32) + _nbytes((rd, D), F32) + _nbytes((ru, FF), BF16) + _nbytes((rd, D), BF16)
    outs = pl.pallas_call(
        functools.partial(_mlp_kernel, cast_next=cast_next),
        grid=(T // tm, nj),
        in_specs=in_specs,
        out_specs=out_specs,
        out_shape=out_shape,
        scratch_shapes=[pltpu.VMEM((tm, D), BF16)],
        compiler_params=_compiler_params(("arbitrary", "arbitrary"), blocks, resident),
        name="mlp",
    )(*args)
    return outs if cast_next else (outs[0], None, None)


def _ple_kernel(h_ref, p_ref, g_ref, gn_ref, wg_ref, wp_ref, *refs, last, n_sub):
    rows = h_ref.shape[0] // n_sub
    subs = [slice(t * rows, (t + 1) * rows) for t in range(n_sub)]
    normed = [_rms(h_ref[rs, :], g_ref[...]).astype(BF16) for rs in subs]
    parts = [(jnp.dot(n, wg_ref[...], preferred_element_type=F32),
              jnp.dot(p_ref[rs, :].astype(BF16), wp_ref[...], preferred_element_type=F32))
             for n, rs in zip(normed, subs)]
    for t, (rs, (gate_logit, pp)) in enumerate(zip(subs, parts)):
        h3 = h_ref[rs, :] + _sigmoid(gate_logit) * pp
        if last:
            refs[0][rs, :] = _rms(h3, gn_ref[...])
        else:
            refs[0][rs, :] = h3
            _emit_u(_rms(h3, gn_ref[...]), t, refs[-1], refs[1], refs[2:-1])


def _ple_call(h, p, g, g_next, wg, wp, layer, B, S, tm, n_sub, last):
    T, D = h.shape
    P = p.shape[2]
    row = lambda i: (i, 0)
    if last:
        out_specs = [pl.BlockSpec((tm, D), row)]
        out_shape = [jax.ShapeDtypeStruct((T, D), F32)]
        out_bytes = _nbytes((tm, D), F32)
        scratch = []
    else:
        u_specs, u_shapes = _u_out_specs(T, D, B, S, tm)
        out_specs = [pl.BlockSpec((tm, D), row)] + u_specs
        out_shape = [jax.ShapeDtypeStruct((T, D), F32)] + u_shapes
        out_bytes = _nbytes((tm, D), F32) + (1 + len(PERM_DILATIONS)) * _nbytes((tm, D), BF16)
        scratch = [_u_scratch(D, tm, n_sub)]
    blocks = _nbytes((tm, D), F32) + _nbytes((tm, P), F32) + out_bytes
    weights = _nbytes((D, D), BF16) + _nbytes((P, D), BF16)
    return pl.pallas_call(
        functools.partial(_ple_kernel, last=last, n_sub=n_sub),
        grid=(T // tm,),
        in_specs=[pl.BlockSpec((tm, D), row),
                  pl.BlockSpec((None, tm, P), lambda i: (layer, i, 0)),
                  pl.BlockSpec((None, 1, D), lambda i: (layer, 0, 0)),
                  pl.BlockSpec((1, D), lambda i: (0, 0)),
                  _resident_spec((None, D, D), lambda i: (layer, 0, 0)),
                  _resident_spec((None, P, D), lambda i: (layer, 0, 0))],
        out_specs=out_specs,
        out_shape=out_shape,
        scratch_shapes=scratch,
        compiler_params=_compiler_params(("parallel",), blocks, weights + 4 * _nbytes((tm, D), F32)),
        name="ple_norm",
    )(h, p, g, g_next, wg, wp)


def _rope_tables(S):
    inv = ROPE_THETA ** (-jnp.arange(ROPE_HALF, dtype=F32) / ROPE_HALF)
    ang = jnp.arange(S, dtype=F32)[:, None] * inv[None, :]
    cos, sin = jnp.cos(ang), jnp.sin(ang)
    ones = jnp.ones((S, HEAD_DIM // 2 - ROPE_HALF), F32)
    zeros = jnp.zeros((S, HEAD_DIM // 2 - ROPE_HALF), F32)
    cos_t = jnp.concatenate([cos, ones, cos, ones], axis=1)
    sin_t = jnp.concatenate([-sin, zeros, sin, zeros], axis=1)
    return cos_t * ATTN_SCALE, sin_t * ATTN_SCALE, cos_t, sin_t


def _residue_major(t, d):
    S = t.shape[0]
    return t.reshape(S // d, d, t.shape[1]).transpose(1, 0, 2).reshape(S, t.shape[1])


def kernel(x, p, g_mix, w_in, b_f, w_gate, b_gate, w_br_a, w_br_b, w_o, g_mlp, w_up, w_down, g_ple, w_ple,
           w_ple_gate, g_final):
    B, S, D = x.shape
    depth = p.shape[0]
    T = B * S
    A, G = ATTN_WIDTH, GROUP_WIDTH
    assert all(w // d == BLOCK and S % (d * BLOCK) == 0 for w, d in DIL_GROUPS) and DIL_GROUPS[0][1] == 1
    assert all(b % a == 0 for a, b in zip((1,) + PERM_DILATIONS, PERM_DILATIONS))
    assert w_in.shape[2] == 3 * A + N_FOX_HEADS and w_gate.shape[2] == 2 * D

    tm_in = min(1024, S)
    tm_row = 256
    tm_mlp, tf_mlp = 1024, 512
    tm_ple, sub_ple = 512, 2
    fox_tile = 256

    fox = N_GROUPS - 1
    w_in_b = w_in.astype(BF16)
    w_qkv = _qkv_prep_call(w_in_b, lambda s: N_GROUPS * (s % 3) + fox * (s // 3), 6, 2, "qkv_prep_main")
    w_dil = [_qkv_prep_call(w_in_b, lambda s, g=g: N_GROUPS * s + g, 3, 2, f"qkv_prep_d{d}")
             for g, (_, d) in enumerate(DIL_GROUPS) if d > 1]
    w_gate_b, b_gate3 = w_gate.astype(BF16), b_gate[:, None, :]
    w_f = jnp.pad(w_in_b[:, :, 3 * A:], ((0, 0), (0, 0), (0, LANES - N_FOX_HEADS)))
    b_fp = jnp.pad(b_f, ((0, 0), (0, LANES - N_FOX_HEADS)))[:, None, :]
    wa, wb, wo = w_br_a.astype(BF16), w_br_b.astype(BF16), w_o.astype(BF16)
    wu, wd = w_up[0].astype(BF16), w_down[0].astype(BF16)
    wpg, wp = w_ple_gate.astype(BF16), w_ple.astype(BF16)
    g_mlp3, g_ple3 = g_mlp[:, None, :], g_ple[:, None, :]
    p3 = p.reshape(depth, T, p.shape[3])
    tables = _rope_tables(S)
    tables_p = [tuple(_residue_major(t, d) for t in tables) for d in PERM_DILATIONS]
    fox_col = 3 * G

    h = x.reshape(T, D)
    u, *u_perm = _rms_call(h, g_mix[0][None, :], B, S, tm_row)
    for layer in range(depth):
        gates = _gate_call(u, w_gate_b, b_gate3, layer, tm_in)
        z = _qkv_call(u, w_qkv, tables, MAIN_KINDS, layer, S, tm_in, "in_qkv")
        ccol, crow = _decay_call(u, w_f, b_fp, layer, B, S)
        dil = [_dil_call(z, 0, 1, B, S)]
        for up, w_g, tp, d in zip(u_perm, w_dil, tables_p, PERM_DILATIONS):
            z_g = _qkv_call(up.reshape(T, D), w_g, tp, DIL_KINDS, layer, S, tm_in, f"in_qkv_d{d}")
            dil.append(_dil_call(z_g, 0, d, B, S))
        yb = _fox_call(z, fox_col, ccol, crow, B, S, fox_tile)
        h = _merge_call([o for o, _ in dil], [l for _, l in dil], yb, gates, h, wa, wb, wo, layer, S, tm_row)
        h, wu, wd = _mlp_call(h, g_mlp3, wu, wd, w_up, w_down, layer, tm_mlp, tf_mlp)
        last = layer == depth - 1
        g_next = g_final[None, :] if last else g_mix[layer + 1][None, :]
        outs = _ple_call(h, p3, g_ple3, g_next, wpg, wp, layer, B, S, tm_ple, sub_ple, last)
        if last:
            return outs[0].reshape(B, S, D)
        h, u, *u_perm = outs
```

```python
import functools

import jax
import jax.numpy as jnp
from jax import lax
from jax.experimental import pallas as pl
from jax.experimental.pallas import tpu as pltpu

HEAD_DIM = 128
BLOCK = 128
DIL_GROUPS = ((128, 1), (512, 4), (2048, 16))
HEADS_PER_DIL = 4
N_DIL_HEADS = HEADS_PER_DIL * len(DIL_GROUPS)
N_FOX_HEADS = 4
N_HEADS = N_DIL_HEADS + N_FOX_HEADS
ATTN_WIDTH = N_HEADS * HEAD_DIM
GROUP_WIDTH = HEADS_PER_DIL * HEAD_DIM
N_GROUPS = ATTN_WIDTH // GROUP_WIDTH
ROPE_THETA = 500000.0
ROPE_DIM = HEAD_DIM // 4
ROPE_HALF = ROPE_DIM // 2
NORM_EPS = 1e-6
ATTN_SCALE = HEAD_DIM ** -0.5

LANES = 128
SUBLANES = 8
BF16_ROW_TILE = 2 * SUBLANES
V7X_VMEM_LIMIT_CAP = 58 * 1024 * 1024

F32 = jnp.float32
BF16 = jnp.bfloat16

PERM_DILATIONS = tuple(d for _, d in DIL_GROUPS if d > 1)


def _nbytes(shape, dtype):
    n = 1
    for s in shape:
        n *= s
    return n * jnp.dtype(dtype).itemsize


def _compiler_params(semantics, pipelined, resident=0):
    need = 2 * pipelined + resident
    assert need <= V7X_VMEM_LIMIT_CAP, (need, V7X_VMEM_LIMIT_CAP)
    return pltpu.CompilerParams(dimension_semantics=semantics, vmem_limit_bytes=V7X_VMEM_LIMIT_CAP)


def _resident_spec(block_shape, index_map):
    return pl.BlockSpec(block_shape, index_map, pipeline_mode=pl.Buffered(1))


def _rms(x, g):
    return x * lax.rsqrt(jnp.mean(x * x, axis=-1, keepdims=True) + NORM_EPS) * g


def _sigmoid(x):
    return 0.5 * jnp.tanh(0.5 * x) + 0.5


def _rope(x, c, s):
    return x * c + pltpu.roll(x, HEAD_DIM // 2, 1) * s


def _emit_u(u, sub, scr_ref, u_ref, perm_refs):
    rows, n_chunks = u.shape[0], u.shape[1] // LANES
    row0 = sub * rows
    u_ref[row0:row0 + rows, :] = u.astype(u_ref.dtype)
    n_sub = scr_ref.shape[0] // (len(perm_refs) * n_chunks)

    def slot(level, c):
        return (level * n_sub + sub) * n_chunks + c

    for c in range(n_chunks):
        scr_ref[slot(0, c)] = u[:, c * LANES:(c + 1) * LANES]
    d_prev = 1
    for level, ref in enumerate(perm_refs):
        d = ref.shape[1]
        f, steps = d // d_prev, rows // d
        for c in range(n_chunks):
            lanes = slice(c * LANES, (c + 1) * LANES)
            for b in range(f):
                part = scr_ref[slot(level, c), pl.ds(b, rows // f, stride=f), :]
                if level + 1 < len(perm_refs):
                    scr_ref[slot(level + 1, c), b * (rows // f):(b + 1) * (rows // f), :] = part
                for rr in range(d // f):
                    ref[0, b * (d // f) + rr, row0 // d:row0 // d + steps, lanes] = (
                        part[rr * steps:(rr + 1) * steps, :].astype(ref.dtype))
        d_prev = d


def _u_scratch(D, tm, n_sub):
    return pltpu.VMEM((len(PERM_DILATIONS) * n_sub * (D // LANES), tm // n_sub, LANES), F32)


def _u_out_specs(T, D, B, S, tm):
    per_seq = S // tm
    specs = [pl.BlockSpec((tm, D), lambda i: (i, 0))]
    shapes = [jax.ShapeDtypeStruct((T, D), BF16)]
    for d in PERM_DILATIONS:
        specs.append(pl.BlockSpec((1, d, tm // d, D), lambda i: (i // per_seq, 0, i % per_seq, 0)))
        shapes.append(jax.ShapeDtypeStruct((B, d, S // d, D), BF16))
    return specs, shapes


def _qkv_prep_kernel(w_ref, o_ref, *, rope_blocks):
    s = pl.program_id(1)

    @pl.when(s < rope_blocks)
    def _rotary_heads():
        lane = lax.broadcasted_iota(jnp.int32, (w_ref.shape[1], HEAD_DIM), 1)
        mid = HEAD_DIM // 2
        for hh in range(o_ref.shape[2] // HEAD_DIM):
            cols = slice(hh * HEAD_DIM, (hh + 1) * HEAD_DIM)
            x = w_ref[0, :, cols].astype(F32)
            up = pltpu.roll(x, HEAD_DIM - ROPE_HALF, 1)
            down = pltpu.roll(x, mid - ROPE_HALF, 1)
            y = jnp.where(lane < ROPE_HALF, x,
                          jnp.where(lane < mid, up, jnp.where(lane < mid + ROPE_HALF, down, x)))
            o_ref[0, :, cols] = y.astype(o_ref.dtype)

    @pl.when(s >= rope_blocks)
    def _plain():
        o_ref[...] = w_ref[...]


def _qkv_prep_call(w_in, src_block, n_blocks, rope_blocks, name):
    depth, D, _ = w_in.shape
    G = GROUP_WIDTH
    return pl.pallas_call(
        functools.partial(_qkv_prep_kernel, rope_blocks=rope_blocks),
        grid=(depth, n_blocks),
        in_specs=[pl.BlockSpec((1, D, G), lambda l, s: (l, 0, src_block(s)))],
        out_specs=pl.BlockSpec((1, D, G), lambda l, s: (l, 0, s)),
        out_shape=jax.ShapeDtypeStruct((depth, D, n_blocks * G), BF16),
        compiler_params=_compiler_params(("parallel", "arbitrary"), 2 * _nbytes((D, G), BF16),
                                         3 * _nbytes((D, HEAD_DIM), F32)),
        name=name,
    )(w_in)


def _rms_kernel(x_ref, g_ref, *refs):
    outs, scr_ref = refs[:-1], refs[-1]
    _emit_u(_rms(x_ref[...], g_ref[...]), 0, scr_ref, outs[0], outs[1:])


def _rms_call(x, g, B, S, tm):
    T, D = x.shape
    specs, shapes = _u_out_specs(T, D, B, S, tm)
    blocks = _nbytes((tm, D), F32) + (1 + len(PERM_DILATIONS)) * _nbytes((tm, D), BF16)
    return pl.pallas_call(
        _rms_kernel,
        grid=(T // tm,),
        in_specs=[pl.BlockSpec((tm, D), lambda i: (i, 0)),
                  pl.BlockSpec((1, D), lambda i: (0, 0))],
        out_specs=specs,
        out_shape=shapes,
        scratch_shapes=[_u_scratch(D, tm, 1)],
        compiler_params=_compiler_params(("parallel",), blocks, 4 * _nbytes((tm, D), F32)),
        name="rms_in",
    )(x, g)


def _project(u_ref, w_ref, o_ref, kinds, chunk, tables):
    cq, sq, ck, sk = tables
    for c, kind in enumerate(kinds):
        acc = jnp.dot(u_ref[...], w_ref[:, c * chunk:(c + 1) * chunk], preferred_element_type=F32)
        if kind in ("rope_q", "rope_k"):
            cos, sin = (cq[...], sq[...]) if kind == "rope_q" else (ck[...], sk[...])
            for hh in range(chunk // HEAD_DIM):
                lo = c * chunk + hh * HEAD_DIM
                x = acc[:, hh * HEAD_DIM:(hh + 1) * HEAD_DIM]
                o_ref[:, lo:lo + HEAD_DIM] = _rope(x, cos, sin).astype(o_ref.dtype)
        else:
            y = acc * ATTN_SCALE if kind == "plain_q" else acc
            o_ref[:, c * chunk:(c + 1) * chunk] = y.astype(o_ref.dtype)


def _gate_kernel(u_ref, w_ref, b_ref, o_ref, *, chunk):
    for c in range(o_ref.shape[1] // chunk):
        cols = slice(c * chunk, (c + 1) * chunk)
        acc = jnp.dot(u_ref[...], w_ref[:, cols], preferred_element_type=F32)
        o_ref[:, cols] = _sigmoid(acc + b_ref[:, cols]).astype(o_ref.dtype)


def _gate_call(u, w, b, layer, tm):
    T, D = u.shape
    N = w.shape[2]
    blocks = _nbytes((tm, D), BF16) + _nbytes((tm, N), BF16) + _nbytes((1, N), F32)
    return pl.pallas_call(
        functools.partial(_gate_kernel, chunk=GROUP_WIDTH // 2),
        grid=(T // tm,),
        in_specs=[pl.BlockSpec((tm, D), lambda i: (i, 0)),
                  _resident_spec((None, D, N), lambda i: (layer, 0, 0)),
                  pl.BlockSpec((None, 1, N), lambda i: (layer, 0, 0))],
        out_specs=pl.BlockSpec((tm, N), lambda i: (i, 0)),
        out_shape=jax.ShapeDtypeStruct((T, N), BF16),
        compiler_params=_compiler_params(("parallel",), blocks,
                                         _nbytes((D, N), BF16) + 3 * _nbytes((tm, GROUP_WIDTH), F32)),
        name="in_gate",
    )(u, w, b)


MAIN_KINDS = ("rope_q", "rope_k", "plain", "plain_q", "plain", "plain")
DIL_KINDS = ("rope_q", "rope_k", "plain")


def _qkv_kernel(u_ref, w_ref, cq_ref, sq_ref, ck_ref, sk_ref, *refs, kinds, with_forget):
    if with_forget:
        wf_ref, bf_ref, o_ref, fl_ref = refs
    else:
        (o_ref,) = refs
    _project(u_ref, w_ref, o_ref, kinds, GROUP_WIDTH, (cq_ref, sq_ref, ck_ref, sk_ref))
    if with_forget:
        fl_ref[...] = jnp.dot(u_ref[...], wf_ref[...], preferred_element_type=F32) + bf_ref[...]


def _qkv_call(u_rows, w, tables_rows, kinds, layer, S, tm, name, forget=None):
    T, D = u_rows.shape
    N = w.shape[2]
    assert N == len(kinds) * GROUP_WIDTH
    pos_blocks = S // tm
    table_spec = pl.BlockSpec((tm, LANES), lambda i: (i % pos_blocks, 0))
    blocks = _nbytes((tm, D), BF16) + _nbytes((tm, N), BF16) + 4 * _nbytes((tm, LANES), F32)
    in_specs = [pl.BlockSpec((tm, D), lambda i: (i, 0)),
                _resident_spec((None, D, N), lambda i: (layer, 0, 0)),
                table_spec, table_spec, table_spec, table_spec]
    out_specs = [pl.BlockSpec((tm, N), lambda i: (i, 0))]
    out_shape = [jax.ShapeDtypeStruct((T, N), BF16)]
    args = [u_rows, w, *tables_rows]
    if forget is not None:
        in_specs += [_resident_spec((None, D, LANES), lambda i: (layer, 0, 0)),
                     _resident_spec((None, 1, LANES), lambda i: (layer, 0, 0))]
        out_specs.append(pl.BlockSpec((tm, LANES), lambda i: (i, 0)))
        out_shape.append(jax.ShapeDtypeStruct((T, LANES), F32))
        args += list(forget)
        blocks += _nbytes((tm, LANES), F32)
    outs = pl.pallas_call(
        functools.partial(_qkv_kernel, kinds=kinds, with_forget=forget is not None),
        grid=(T // tm,),
        in_specs=in_specs,
        out_specs=out_specs,
        out_shape=out_shape,
        compiler_params=_compiler_params(("parallel",), blocks,
                                         _nbytes((D, N + LANES), BF16) + 3 * _nbytes((tm, GROUP_WIDTH), F32)),
        name=name,
    )(*args)
    return outs if forget is not None else outs[0]


def _decay_kernel(fl_ref, ccol_ref, crow_ref):
    fl = fl_ref[...]
    ls = jnp.minimum(fl, 0.0) - jnp.log1p(jnp.exp(-jnp.abs(fl)))
    ri = lax.broadcasted_iota(jnp.int32, (BLOCK, BLOCK), 0)
    ci = lax.broadcasted_iota(jnp.int32, (BLOCK, BLOCK), 1)
    tri = (ci <= ri).astype(F32)
    carry = jnp.zeros((1, LANES), F32)
    for n in range(fl.shape[0] // BLOCK):
        rows = slice(n * BLOCK, (n + 1) * BLOCK)
        blk = jnp.dot(tri, ls[rows], precision=lax.Precision.HIGHEST, preferred_element_type=F32) + carry
        for h in range(N_FOX_HEADS):
            ccol_ref[0, h, rows, :] = jnp.broadcast_to(blk[:, h:h + 1], (BLOCK, LANES))
        crow_ref[0, :, rows] = blk.T[0:SUBLANES, :]
        carry = blk[BLOCK - 1:BLOCK, :]


def _decay_call(fl, B, S):
    blocks = (1 + N_FOX_HEADS) * _nbytes((S, LANES), F32) + _nbytes((SUBLANES, S), F32)
    return pl.pallas_call(
        _decay_kernel,
        grid=(B,),
        in_specs=[pl.BlockSpec((S, LANES), lambda b_: (b_, 0))],
        out_specs=[pl.BlockSpec((1, N_FOX_HEADS, S, LANES), lambda b_: (b_, 0, 0, 0)),
                   pl.BlockSpec((1, SUBLANES, S), lambda b_: (b_, 0, 0))],
        out_shape=[jax.ShapeDtypeStruct((B, N_FOX_HEADS, S, LANES), F32),
                   jax.ShapeDtypeStruct((B, SUBLANES, S), F32)],
        compiler_params=_compiler_params(("parallel",), blocks, 4 * _nbytes((S, LANES), F32)),
        name="fox_decay",
    )(fl)


def _dil_kernel(q_ref, k_ref, v_ref, o_ref, lse_ref, *, dilation, nb, diag_group, band_group):
    qi = lax.broadcasted_iota(jnp.int32, (BLOCK, 2 * BLOCK), 0)
    ki = lax.broadcasted_iota(jnp.int32, (BLOCK, 2 * BLOCK), 1)
    band = (ki >= qi) & (ki <= qi + BLOCK)
    qd = lax.broadcasted_iota(jnp.int32, (BLOCK, BLOCK), 0)
    kd = lax.broadcasted_iota(jnp.int32, (BLOCK, BLOCK), 1)
    diag = kd <= qd

    def run(units, klen, mask):
        jobs = [(r, q0, k0, h) for (r, q0, k0) in units for h in range(HEADS_PER_DIL)]
        scores = []
        for r, q0, k0, h in jobs:
            cs = pl.ds(h * HEAD_DIM, HEAD_DIM)
            s = lax.dot_general(q_ref[0, r, pl.ds(q0, BLOCK), cs], k_ref[0, r, pl.ds(k0, klen), cs],
                                (((1,), (1,)), ((), ())), preferred_element_type=F32)
            scores.append(jnp.where(mask, s, -jnp.inf))
        probs = []
        for s in scores:
            m = jnp.max(s, axis=-1, keepdims=True)
            probs.append((jnp.exp(s - m).astype(BF16), m))
        ones = jnp.ones((klen, HEAD_DIM), BF16)
        for (r, q0, k0, h), (p, m) in zip(jobs, probs):
            v = v_ref[0, r, pl.ds(k0, klen), pl.ds(h * HEAD_DIM, HEAD_DIM)]
            acc = jnp.dot(p, jnp.concatenate([v, ones], axis=1), preferred_element_type=F32)
            l = acc[:, HEAD_DIM:]
            out_rows = pl.ds(q0 * dilation + r, BLOCK, stride=dilation) if dilation > 1 else pl.ds(q0, BLOCK)
            o_ref[0, h, out_rows, :] = acc[:, :HEAD_DIM] * (1.0 / l)
            lse_ref[0, h, out_rows, :] = jnp.broadcast_to(m, (BLOCK, HEAD_DIM)) + jnp.log(l)

    def loop(trips, body):
        if trips == 1:
            body(0)
        elif trips > 1:
            lax.fori_loop(0, trips, lambda t, c: (body(t), c)[1], 0)

    loop(dilation // diag_group,
         lambda g: run([(g * diag_group + t, 0, 0) for t in range(diag_group)], BLOCK, diag))

    def band_blocks(r):
        def body(g):
            units = []
            for t in range(band_group):
                q0 = (1 + g * band_group + t) * BLOCK
                k0 = q0 - BLOCK
                if not isinstance(q0, int):
                    q0, k0 = pl.multiple_of(q0, BLOCK), pl.multiple_of(k0, BLOCK)
                units.append((r, q0, k0))
            run(units, 2 * BLOCK, band)

        loop((nb - 1) // band_group, body)

    loop(dilation, band_blocks)


def _dil_call(z, col0, dilation, B, S):
    T, N = z.shape
    L = S // dilation
    nb = L // BLOCK
    zv = z.reshape(B, dilation, L, N)
    cb = col0 // GROUP_WIDTH

    def spec(c):
        return pl.BlockSpec((1, dilation, L, GROUP_WIDTH), lambda b_: (b_, 0, 0, cb + c))

    out_spec = pl.BlockSpec((1, HEADS_PER_DIL, S, HEAD_DIM), lambda b_: (b_, 0, 0, 0))
    out_sds = jax.ShapeDtypeStruct((B, HEADS_PER_DIL, S, HEAD_DIM), F32)
    blocks = 3 * _nbytes((S, GROUP_WIDTH), BF16) + 2 * _nbytes((S, GROUP_WIDTH), F32)
    return pl.pallas_call(
        functools.partial(_dil_kernel, dilation=dilation, nb=nb, diag_group=min(dilation, 4),
                          band_group=3 if (nb - 1) % 3 == 0 else 1),
        grid=(B,),
        in_specs=[spec(0), spec(1), spec(2)],
        out_specs=[out_spec, out_spec],
        out_shape=[out_sds, out_sds],
        compiler_params=_compiler_params(("parallel",), blocks),
        name=f"dilated_d{dilation}",
    )(zv, zv, zv)


def _fox_kernel(q_ref, k_ref, v_ref, cq_ref, ck_ref, o_ref, acc_ref, m_ref, *, tile):
    i = pl.program_id(1)
    qd = lax.broadcasted_iota(jnp.int32, (tile, tile), 0)
    kd = lax.broadcasted_iota(jnp.int32, (tile, tile), 1)
    causal = kd <= qd
    heads = [pl.ds(h * HEAD_DIM, HEAD_DIM) for h in range(N_FOX_HEADS)]
    for h in range(N_FOX_HEADS):
        m_ref[h] = jnp.full((tile, LANES), -jnp.inf, F32)
        acc_ref[h] = jnp.zeros((tile, 2 * HEAD_DIM), F32)

    def step(k0, width, masked):
        k0 = pl.multiple_of(k0, tile)
        slabs = [slice(c * LANES, (c + 1) * LANES) for c in range(width // LANES)]
        ones = jnp.ones((width, HEAD_DIM), BF16)
        logits = []
        for h, cs in enumerate(heads):
            s = lax.dot_general(q_ref[0, :, cs], k_ref[0, pl.ds(k0, width), cs], (((1,), (1,)), ((), ())),
                                preferred_element_type=F32)
            s = s - ck_ref[0, h:h + 1, pl.ds(k0, width)]
            logits.append(jnp.where(causal, s, -jnp.inf) if masked else s)
        probs = []
        for h, s in enumerate(logits):
            cq = cq_ref[0, h]
            m_old = m_ref[h]
            row_max = jnp.broadcast_to(jnp.max(s, axis=-1, keepdims=True), (tile, LANES))
            m_new = jnp.maximum(m_old, row_max + cq)
            m_ref[h] = m_new
            shift = cq - m_new
            p = jnp.concatenate([jnp.exp(s[:, sl] + shift) for sl in slabs], axis=1).astype(BF16)
            probs.append((p, jnp.exp(m_old - m_new)))
        for h, (cs, (p, alpha)) in enumerate(zip(heads, probs)):
            v_ones = jnp.concatenate([v_ref[0, pl.ds(k0, width), cs], ones], axis=1)
            pv = jnp.dot(p, v_ones, preferred_element_type=F32)
            for sl in (slice(0, HEAD_DIM), slice(HEAD_DIM, 2 * HEAD_DIM)):
                acc_ref[h, :, sl] = alpha * acc_ref[h, :, sl] + pv[:, sl]

    lax.fori_loop(0, i // 2, lambda t, c: (step(t * 2 * tile, 2 * tile, False), c)[1], 0)

    @pl.when(i % 2 == 1)
    def _odd():
        step((i - 1) * tile, tile, False)

    step(i * tile, tile, True)
    for h, cs in enumerate(heads):
        acc = acc_ref[h]
        o_ref[0, :, cs] = (acc[:, :HEAD_DIM] * (1.0 / acc[:, HEAD_DIM:])).astype(o_ref.dtype)


def _fox_call(z, col0, ccol, crow, B, S, tile):
    T, N = z.shape
    zv = z.reshape(B, S, N)
    base = col0 // GROUP_WIDTH
    blocks = (_nbytes((tile, GROUP_WIDTH), BF16) * 2 + 2 * _nbytes((S, GROUP_WIDTH), BF16)
              + N_FOX_HEADS * _nbytes((tile, LANES), F32) + _nbytes((SUBLANES, S), F32))
    scratch = [pltpu.VMEM((N_FOX_HEADS, tile, 2 * HEAD_DIM), F32), pltpu.VMEM((N_FOX_HEADS, tile, LANES), F32)]
    out = pl.pallas_call(
        functools.partial(_fox_kernel, tile=tile),
        grid=(B, S // tile),
        in_specs=[pl.BlockSpec((1, tile, GROUP_WIDTH), lambda b_, i: (b_, i, base)),
                  pl.BlockSpec((1, S, GROUP_WIDTH), lambda b_, i: (b_, 0, base + 1)),
                  pl.BlockSpec((1, S, GROUP_WIDTH), lambda b_, i: (b_, 0, base + 2)),
                  pl.BlockSpec((1, N_FOX_HEADS, tile, LANES), lambda b_, i: (b_, 0, i, 0)),
                  pl.BlockSpec((1, SUBLANES, S), lambda b_, i: (b_, 0, 0))],
        out_specs=pl.BlockSpec((1, tile, GROUP_WIDTH), lambda b_, i: (b_, i, 0)),
        out_shape=jax.ShapeDtypeStruct((B, S, GROUP_WIDTH), BF16),
        scratch_shapes=scratch,
        compiler_params=_compiler_params(("parallel", "arbitrary"), blocks,
                                         N_FOX_HEADS * (2 * _nbytes((tile, 2 * HEAD_DIM), F32)
                                                        + _nbytes((tile, LANES), F32) + 2 * _nbytes((tile, tile), F32))),
        name="fox_attention",
    )(zv, zv, zv, ccol, crow)
    return out.reshape(T, GROUP_WIDTH)


def _merge_kernel(o0_ref, o1_ref, o2_ref, l0_ref, l1_ref, l2_ref, yb_ref, ga_ref, gb_ref, h_ref,
                  wa_ref, wb_ref, wo_ref, h1_ref):
    yb_p = jnp.dot(yb_ref[...], wb_ref[...], preferred_element_type=F32)
    heads = []
    for hd in range(HEADS_PER_DIL):
        l0, l1, l2 = l0_ref[0, hd], l1_ref[0, hd], l2_ref[0, hd]
        mx = jnp.maximum(jnp.maximum(l0, l1), l2)
        e0, e1, e2 = jnp.exp(l0 - mx), jnp.exp(l1 - mx), jnp.exp(l2 - mx)
        inv = 1.0 / (e0 + e1 + e2)
        ya_h = (e0 * inv) * o0_ref[0, hd] + (e1 * inv) * o1_ref[0, hd] + (e2 * inv) * o2_ref[0, hd]
        heads.append(ya_h.astype(BF16))
    ya = jnp.concatenate(heads, axis=1)
    ya_p = jnp.dot(ya, wa_ref[...], preferred_element_type=F32)
    merged = (ga_ref[...].astype(F32) * ya_p + gb_ref[...].astype(F32) * yb_p).astype(BF16)
    for c in range(h1_ref.shape[1] // GROUP_WIDTH):
        cols = slice(c * GROUP_WIDTH, (c + 1) * GROUP_WIDTH)
        h1_ref[:, cols] = h_ref[:, cols] + jnp.dot(merged, wo_ref[:, cols], preferred_element_type=F32)


def _merge_call(os_, ls_, yb, z, h, wa, wb, wo, layer, S, tm):
    T, D = h.shape
    row = lambda i: (i, 0)
    per_seq = S // tm
    hw = pl.BlockSpec((1, HEADS_PER_DIL, tm, HEAD_DIM), lambda i: (i // per_seq, 0, i % per_seq, 0))
    blocks = (6 * _nbytes((tm, GROUP_WIDTH), F32) + _nbytes((tm, GROUP_WIDTH), BF16) + 2 * _nbytes((tm, D), BF16)
              + 2 * _nbytes((tm, D), F32))
    weights = 2 * _nbytes((GROUP_WIDTH, D), BF16) + _nbytes((D, D), BF16)
    return pl.pallas_call(
        _merge_kernel,
        grid=(T // tm,),
        in_specs=[hw, hw, hw, hw, hw, hw, pl.BlockSpec((tm, GROUP_WIDTH), row),
                  pl.BlockSpec((tm, D), lambda i: (i, 0)),
                  pl.BlockSpec((tm, D), lambda i: (i, 1)),
                  pl.BlockSpec((tm, D), row),
                  _resident_spec((None, GROUP_WIDTH, D), lambda i: (layer, 0, 0)),
                  _resident_spec((None, GROUP_WIDTH, D), lambda i: (layer, 0, 0)),
                  _resident_spec((None, D, D), lambda i: (layer, 0, 0))],
        out_specs=pl.BlockSpec((tm, D), row),
        out_shape=jax.ShapeDtypeStruct((T, D), F32),
        compiler_params=_compiler_params(("parallel",), blocks, weights + 3 * _nbytes((tm, D), F32)),
        name="merge_wo",
    )(*os_, *ls_, yb, z, z, h, wa, wb, wo)


def _mlp_kernel(h_ref, g_ref, wu_ref, wd_ref, *refs, cast_next):
    if cast_next:
        next_wu_ref, next_wd_ref, o_ref, cast_wu_ref, cast_wd_ref, m_ref = refs
        cast_wu_ref[...] = next_wu_ref[...].astype(cast_wu_ref.dtype)
        cast_wd_ref[...] = next_wd_ref[...].astype(cast_wd_ref.dtype)
    else:
        o_ref, m_ref = refs
    j = pl.program_id(1)

    @pl.when(j == 0)
    def _init():
        h = h_ref[...]
        m_ref[...] = _rms(h, g_ref[...]).astype(m_ref.dtype)
        o_ref[...] = h

    a = jnp.dot(m_ref[...], wu_ref[...], preferred_element_type=F32)
    a = jnp.square(jnp.maximum(a, 0.0)).astype(BF16)
    o_ref[...] += jnp.dot(a, wd_ref[...], preferred_element_type=F32)


def _cast_chunk(rows, steps):
    assert rows % steps == 0
    per_step = rows // steps
    share = max(1, BF16_ROW_TILE // per_step)
    return per_step * share, share


def _mlp_call(h, g, wu, wd, w_up, w_down, layer, tm, tf):
    T, D = h.shape
    FF = wu.shape[1]
    nj = FF // tf
    cast_next = layer + 1 < w_up.shape[0]
    blocks = 2 * _nbytes((tm, D), F32) + 2 * _nbytes((D, tf), BF16)
    resident = _nbytes((tm, D), BF16) + _nbytes((tm, tf), F32) + _nbytes((tm, tf), BF16)
    in_specs = [pl.BlockSpec((tm, D), lambda i, j: (i, 0)),
                pl.BlockSpec((None, 1, D), lambda i, j: (layer, 0, 0)),
                pl.BlockSpec((D, tf), lambda i, j: (0, j)),
                pl.BlockSpec((tf, D), lambda i, j: (j, 0))]
    out_specs = [pl.BlockSpec((tm, D), lambda i, j: (i, 0))]
    out_shape = [jax.ShapeDtypeStruct((T, D), F32)]
    args = [h, g, wu, wd]
    if cast_next:
        steps = (T // tm) * nj
        (ru, su), (rd, sd) = _cast_chunk(D, steps), _cast_chunk(FF, steps)
        in_specs += [pl.BlockSpec((None, ru, FF), lambda i, j: (layer + 1, (i * nj + j) // su, 0)),
                     pl.BlockSpec((None, rd, D), lambda i, j: (layer + 1, (i * nj + j) // sd, 0))]
        out_specs += [pl.BlockSpec((ru, FF), lambda i, j: ((i * nj + j) // su, 0)),
                      pl.BlockSpec((rd, D), lambda i, j: ((i * nj + j) // sd, 0))]
        out_shape += [jax.ShapeDtypeStruct((D, FF), BF16), jax.ShapeDtypeStruct((FF, D), BF16)]
        args += [w_up, w_down]
        blocks += _nbytes((ru, FF), F32) + _nbytes((rd, D), F32) + _nbytes((ru, FF), BF16) + _nbytes((rd, D), BF16)
    outs = pl.pallas_call(
        functools.partial(_mlp_kernel, cast_next=cast_next),
        grid=(T // tm, nj),
        in_specs=in_specs,
        out_specs=out_specs,
        out_shape=out_shape,
        scratch_shapes=[pltpu.VMEM((tm, D), BF16)],
        compiler_params=_compiler_params(("arbitrary", "arbitrary"), blocks, resident),
        name="mlp",
    )(*args)
    return outs if cast_next else (outs[0], None, None)


def _ple_kernel(h_ref, p_ref, g_ref, gn_ref, wg_ref, wp_ref, *refs, last, n_sub):
    rows = h_ref.shape[0] // n_sub
    subs = [slice(t * rows, (t + 1) * rows) for t in range(n_sub)]
    normed = [_rms(h_ref[rs, :], g_ref[...]).astype(BF16) for rs in subs]
    parts = [(jnp.dot(n, wg_ref[...], preferred_element_type=F32),
              jnp.dot(p_ref[rs, :].astype(BF16), wp_ref[...], preferred_element_type=F32))
             for n, rs in zip(normed, subs)]
    for t, (rs, (gate_logit, pp)) in enumerate(zip(subs, parts)):
        h3 = h_ref[rs, :] + _sigmoid(gate_logit) * pp
        if last:
            refs[0][rs, :] = _rms(h3, gn_ref[...])
        else:
            refs[0][rs, :] = h3
            _emit_u(_rms(h3, gn_ref[...]), t, refs[-1], refs[1], refs[2:-1])


def _ple_call(h, p, g, g_next, wg, wp, layer, B, S, tm, n_sub, last):
    T, D = h.shape
    P = p.shape[2]
    row = lambda i: (i, 0)
    if last:
        out_specs = [pl.BlockSpec((tm, D), row)]
        out_shape = [jax.ShapeDtypeStruct((T, D), F32)]
        out_bytes = _nbytes((tm, D), F32)
        scratch = []
    else:
        u_specs, u_shapes = _u_out_specs(T, D, B, S, tm)
        out_specs = [pl.BlockSpec((tm, D), row)] + u_specs
        out_shape = [jax.ShapeDtypeStruct((T, D), F32)] + u_shapes
        out_bytes = _nbytes((tm, D), F32) + (1 + len(PERM_DILATIONS)) * _nbytes((tm, D), BF16)
        scratch = [_u_scratch(D, tm, n_sub)]
    blocks = _nbytes((tm, D), F32) + _nbytes((tm, P), F32) + out_bytes
    weights = _nbytes((D, D), BF16) + _nbytes((P, D), BF16)
    return pl.pallas_call(
        functools.partial(_ple_kernel, last=last, n_sub=n_sub),
        grid=(T // tm,),
        in_specs=[pl.BlockSpec((tm, D), row),
                  pl.BlockSpec((None, tm, P), lambda i: (layer, i, 0)),
                  pl.BlockSpec((None, 1, D), lambda i: (layer, 0, 0)),
                  pl.BlockSpec((1, D), lambda i: (0, 0)),
                  _resident_spec((None, D, D), lambda i: (layer, 0, 0)),
                  _resident_spec((None, P, D), lambda i: (layer, 0, 0))],
        out_specs=out_specs,
        out_shape=out_shape,
        scratch_shapes=scratch,
        compiler_params=_compiler_params(("parallel",), blocks, weights + 4 * _nbytes((tm, D), F32)),
        name="ple_norm",
    )(h, p, g, g_next, wg, wp)


def _rope_tables(S):
    inv = ROPE_THETA ** (-jnp.arange(ROPE_HALF, dtype=F32) / ROPE_HALF)
    ang = jnp.arange(S, dtype=F32)[:, None] * inv[None, :]
    cos, sin = jnp.cos(ang), jnp.sin(ang)
    ones = jnp.ones((S, HEAD_DIM // 2 - ROPE_HALF), F32)
    zeros = jnp.zeros((S, HEAD_DIM // 2 - ROPE_HALF), F32)
    cos_t = jnp.concatenate([cos, ones, cos, ones], axis=1)
    sin_t = jnp.concatenate([-sin, zeros, sin, zeros], axis=1)
    return cos_t * ATTN_SCALE, sin_t * ATTN_SCALE, cos_t, sin_t


def _residue_major(t, d):
    S = t.shape[0]
    return t.reshape(S // d, d, t.shape[1]).transpose(1, 0, 2).reshape(S, t.shape[1])


def kernel(x, p, g_mix, w_in, b_f, w_gate, b_gate, w_br_a, w_br_b, w_o, g_mlp, w_up, w_down, g_ple, w_ple,
           w_ple_gate, g_final):
    B, S, D = x.shape
    depth = p.shape[0]
    T = B * S
    A, G = ATTN_WIDTH, GROUP_WIDTH
    assert all(w // d == BLOCK and S % (d * BLOCK) == 0 for w, d in DIL_GROUPS) and DIL_GROUPS[0][1] == 1
    assert all(b % a == 0 for a, b in zip((1,) + PERM_DILATIONS, PERM_DILATIONS))
    assert w_in.shape[2] == 3 * A + N_FOX_HEADS and w_gate.shape[2] == 2 * D

    tm_in = min(1024, S)
    tm_row = 256
    tm_mlp, tf_mlp = 1024, 512
    tm_ple, sub_ple = 512, 2
    fox_tile = 256

    fox = N_GROUPS - 1
    w_in_b = w_in.astype(BF16)
    w_qkv = _qkv_prep_call(w_in_b, lambda s: N_GROUPS * (s % 3) + fox * (s // 3), 6, 2, "qkv_prep_main")
    w_dil = [_qkv_prep_call(w_in_b, lambda s, g=g: N_GROUPS * s + g, 3, 2, f"qkv_prep_d{d}")
             for g, (_, d) in enumerate(DIL_GROUPS) if d > 1]
    w_gate_b, b_gate3 = w_gate.astype(BF16), b_gate[:, None, :]
    w_f = jnp.pad(w_in_b[:, :, 3 * A:], ((0, 0), (0, 0), (0, LANES - N_FOX_HEADS)))
    b_fp = jnp.pad(b_f, ((0, 0), (0, LANES - N_FOX_HEADS)))[:, None, :]
    wa, wb, wo = w_br_a.astype(BF16), w_br_b.astype(BF16), w_o.astype(BF16)
    wu, wd = w_up[0].astype(BF16), w_down[0].astype(BF16)
    wpg, wp = w_ple_gate.astype(BF16), w_ple.astype(BF16)
    g_mlp3, g_ple3 = g_mlp[:, None, :], g_ple[:, None, :]
    p3 = p.reshape(depth, T, p.shape[3])
    tables = _rope_tables(S)
    tables_p = [tuple(_residue_major(t, d) for t in tables) for d in PERM_DILATIONS]
    fox_col = 3 * G

    h = x.reshape(T, D)
    u, *u_perm = _rms_call(h, g_mix[0][None, :], B, S, tm_row)
    for layer in range(depth):
        gates = _gate_call(u, w_gate_b, b_gate3, layer, tm_in)
        z, fl = _qkv_call(u, w_qkv, tables, MAIN_KINDS, layer, S, tm_in, "in_qkv", forget=(w_f, b_fp))
        ccol, crow = _decay_call(fl, B, S)
        dil = [_dil_call(z, 0, 1, B, S)]
        for up, w_g, tp, d in zip(u_perm, w_dil, tables_p, PERM_DILATIONS):
            z_g = _qkv_call(up.reshape(T, D), w_g, tp, DIL_KINDS, layer, S, tm_in, f"in_qkv_d{d}")
            dil.append(_dil_call(z_g, 0, d, B, S))
        yb = _fox_call(z, fox_col, ccol, crow, B, S, fox_tile)
        h = _merge_call([o for o, _ in dil], [l for _, l in dil], yb, gates, h, wa, wb, wo, layer, S, tm_row)
        h, wu, wd = _mlp_call(h, g_mlp3, wu, wd, w_up, w_down, layer, tm_mlp, tf_mlp)
        last = layer == depth - 1
        g_next = g_final[None, :] if last else g_mix[layer + 1][None, :]
        outs = _ple_call(h, p3, g_ple3, g_next, wpg, wp, layer, B, S, tm_ple, sub_ple, last)
        if last:
            return outs[0].reshape(B, S, D)
        h, u, *u_perm = outs
```
